```python
import jax, jax.numpy as jnp
from jax import lax
import numpy as np

D_MODEL = 1024
BATCH = 16
SEQ = 2048
DEPTH = 2

CHUNK = 64
N_A_LAYERS = DEPTH // 2
N_B_LAYERS = DEPTH - N_A_LAYERS
EPS = 1e-6

MA_HEADS = 8
MA_QK = 64
MA_V = D_MODEL // MA_HEADS
MA_CONV = 4
MA_QK_COLS = 2 * MA_HEADS * MA_QK
MA_PROJ = MA_QK_COLS + 2 * D_MODEL + 2 * MA_HEADS

SB_HEADS = 16
SB_DIM = D_MODEL // SB_HEADS
SB_BLOCK = 128

PEER_HEADS = 8
PEER_NKEYS = 128
PEER_EXPERTS = PEER_NKEYS * PEER_NKEYS
PEER_QDIM = 256
PEER_TOPK = 16
PEER_BLOCK = 128

kernel_name = 'hybrid_mlstm_stickbreaking_peer'


def rmsnorm(x, g):
    xf = x.astype(jnp.float32)
    y = xf * lax.rsqrt(jnp.mean(xf * xf, axis=-1, keepdims=True) + EPS)
    return (y * g.astype(jnp.float32)).astype(x.dtype)


def ada_mod(c, w, b):
    return jnp.einsum('bd,de->be', jax.nn.silu(c), w) + b


def modulate(h, shift, scale):
    return h * (1.0 + scale[:, None, :]) + shift[:, None, :]


def causal_conv(x, w):
    width, ch = w.shape
    return lax.conv_general_dilated(x, w[:, None, :], window_strides=(1,), padding=[(width - 1, 0)],
                                    dimension_numbers=('NWC', 'WIO', 'NWC'), feature_group_count=ch)


def mlstm_mixer(h, w_in, conv_w, b_if, hnorm_g, w_out):
    bsz, seq, _ = h.shape
    nc = seq // CHUNK
    proj = jnp.einsum('bsd,de->bse', h, w_in)
    qk, v, o, gates = jnp.split(proj, [MA_QK_COLS, MA_QK_COLS + D_MODEL, MA_QK_COLS + 2 * D_MODEL], axis=-1)
    qk = jax.nn.silu(causal_conv(qk, conv_w))
    q, k = jnp.split(qk, 2, axis=-1)
    gates = (gates + b_if).astype(jnp.float32)
    i_pre, f_pre = jnp.split(gates, 2, axis=-1)

    def to_chunks(t, d):
        return t.reshape(bsz, nc, CHUNK, MA_HEADS, d).transpose(0, 3, 1, 2, 4)

    q = to_chunks(q, MA_QK).astype(jnp.float32) * (MA_QK ** -0.5)
    k = to_chunks(k, MA_QK).astype(jnp.float32)
    v = to_chunks(v, MA_V).astype(jnp.float32)
    i_pre = i_pre.reshape(bsz, nc, CHUNK, MA_HEADS).transpose(0, 3, 1, 2)
    log_f = jax.nn.log_sigmoid(f_pre).reshape(bsz, nc, CHUNK, MA_HEADS).transpose(0, 3, 1, 2)
    b = jnp.cumsum(log_f, axis=-1)
    b_last = b[..., -1]

    w_state = b_last[..., None] - b + i_pre
    m_loc = jnp.max(w_state, axis=-1)
    e_state = jnp.exp(w_state - m_loc[..., None])
    c_loc = jnp.einsum('bhcl,bhclv,bhclk->bhcvk', e_state, v, k)
    n_loc = jnp.einsum('bhcl,bhclk->bhck', e_state, k)

    def step(carry, inp):
        c_st, n_st, m_st = carry
        cl, nl, ml, bl = inp
        m_new = jnp.maximum(bl + m_st, ml)
        a = jnp.exp(bl + m_st - m_new)
        r = jnp.exp(ml - m_new)
        c_new = a[..., None, None] * c_st + r[..., None, None] * cl
        n_new = a[..., None] * n_st + r[..., None] * nl
        return (c_new, n_new, m_new), (c_st, n_st, m_st)

    init = (jnp.zeros((bsz, MA_HEADS, MA_V, MA_QK), jnp.float32),
            jnp.zeros((bsz, MA_HEADS, MA_QK), jnp.float32),
            jnp.zeros((bsz, MA_HEADS), jnp.float32))
    xs = (c_loc.transpose(2, 0, 1, 3, 4), n_loc.transpose(2, 0, 1, 3),
          m_loc.transpose(2, 0, 1), b_last.transpose(2, 0, 1))
    _, (c_prev, n_prev, m_prev) = lax.scan(step, init, xs)
    c_prev = c_prev.transpose(1, 2, 0, 3, 4)
    n_prev = n_prev.transpose(1, 2, 0, 3)
    m_prev = m_prev.transpose(1, 2, 0)

    causal = jnp.tril(jnp.ones((CHUNK, CHUNK), dtype=bool))
    log_d = jnp.where(causal, b[..., :, None] - b[..., None, :] + i_pre[..., None, :], -jnp.inf)
    inter_log = b + m_prev[..., None]
    m_t = jnp.maximum(inter_log, jnp.max(log_d, axis=-1))
    inter_w = jnp.exp(inter_log - m_t)
    s_qk = jnp.einsum('bhcld,bhcsd->bhcls', q, k) * jnp.exp(log_d - m_t[..., None])
    num = jnp.einsum('bhcls,bhcsv->bhclv', s_qk, v) + inter_w[..., None] * jnp.einsum('bhcvk,bhclk->bhclv', c_prev, q)
    den = jnp.sum(s_qk, axis=-1) + inter_w * jnp.einsum('bhck,bhclk->bhcl', n_prev, q)
    hh = num / jnp.maximum(jnp.abs(den), jnp.exp(-m_t))[..., None]
    hh = hh.transpose(0, 2, 3, 1, 4).reshape(bsz, seq, MA_HEADS, MA_V)
    hh = rmsnorm(hh, hnorm_g) * jax.nn.sigmoid(o.astype(jnp.float32)).reshape(bsz, seq, MA_HEADS, MA_V)
    return jnp.einsum('bse,ed->bsd', hh.reshape(bsz, seq, D_MODEL).astype(h.dtype), w_out)


def shared_kv(x, c, kv_ada_w, kv_ada_b, kv_norm_g, kv_w, k_norm_g):
    bsz, seq, _ = x.shape
    shift, scale = jnp.split(ada_mod(c, kv_ada_w, kv_ada_b), 2, axis=-1)
    h = modulate(rmsnorm(x, kv_norm_g), shift, scale)
    kv = jnp.einsum('bsd,de->bse', h, kv_w)
    k, v = jnp.split(kv, 2, axis=-1)
    k = rmsnorm(k.reshape(bsz, seq, SB_HEADS, SB_DIM), k_norm_g).transpose(0, 2, 1, 3)
    v = v.reshape(bsz, seq, SB_HEADS, SB_DIM).transpose(0, 2, 1, 3)
    return k, v


def stick_breaking_mixer(h, k, v, w_q, q_norm_g, w_out):
    bsz, seq, _ = h.shape
    q = jnp.einsum('bsd,de->bse', h, w_q).reshape(bsz, seq, SB_HEADS, SB_DIM)
    q = (rmsnorm(q, q_norm_g) * (SB_DIM ** -0.5)).transpose(0, 2, 1, 3)
    outs = []
    for blk in range(seq // SB_BLOCK):
        t0 = blk * SB_BLOCK
        end = t0 + SB_BLOCK
        z = jnp.einsum('bhtd,bhsd->bhts', q[:, :, t0:end], k[:, :, :end]).astype(jnp.float32)
        strict = jnp.arange(end)[None, :] < (t0 + jnp.arange(SB_BLOCK))[:, None]
        log1m = jnp.where(strict, -jax.nn.softplus(z), 0.0)
        between = lax.cumsum(log1m, axis=3, reverse=True) - log1m
        a = jnp.where(strict, jnp.exp(jax.nn.log_sigmoid(z) + between), 0.0)
        outs.append(jnp.einsum('bhts,bhsd->bhtd', a, v[:, :, :end].astype(jnp.float32)))
    o = jnp.concatenate(outs, axis=2).transpose(0, 2, 1, 3).reshape(bsz, seq, D_MODEL)
    return jnp.einsum('bse,ed->bsd', o.astype(h.dtype), w_out)


def peer_ffn(h, w_q, sub_keys, peer_u, peer_v):
    bsz, seq, d = h.shape
    xt = h.reshape(bsz * seq // PEER_BLOCK, PEER_BLOCK, d)

    def block(xb):
        q = jnp.einsum('td,de->te', xb, w_q).reshape(PEER_BLOCK, PEER_HEADS, 2, PEER_QDIM // 2)
        s = jnp.einsum('thpk,pnk->thpn', q, sub_keys).astype(jnp.float32)
        s_top, i_top = lax.top_k(s, PEER_TOPK)
        cand = (s_top[:, :, 0, :, None] + s_top[:, :, 1, None, :]).reshape(PEER_BLOCK, PEER_HEADS, PEER_TOPK * PEER_TOPK)
        cand_idx = (i_top[:, :, 0, :, None] * PEER_NKEYS + i_top[:, :, 1, None, :]).reshape(PEER_BLOCK, PEER_HEADS, PEER_TOPK * PEER_TOPK)
        g_top, pos = lax.top_k(cand, PEER_TOPK)
        expert = jnp.take_along_axis(cand_idx, pos, axis=-1)
        g = jax.nn.softmax(g_top, axis=-1)
        u = jnp.take(peer_u, expert, axis=0)
        act = jax.nn.gelu(jnp.einsum('thkd,td->thk', u, xb).astype(jnp.float32), approximate=False)
        vv = jnp.take(peer_v, expert, axis=0)
        return jnp.einsum('thk,thkd->td', (g * act).astype(xb.dtype), vv)

    return lax.map(block, xt).reshape(bsz, seq, d)


def setup_inputs(seed: int = 0) -> dict:
    key = jax.random.key(seed)
    ks = jax.random.split(key, 24)
    f32 = jnp.float32
    d = D_MODEL

    def nrm(k, shape, scale):
        return jax.random.normal(k, shape, f32) * scale

    def gain(k, shape):
        return 1.0 + 0.02 * jax.random.normal(k, shape, f32)

    if_base = jnp.concatenate([jnp.zeros((MA_HEADS,), f32), jnp.linspace(3.0, 6.0, MA_HEADS, dtype=f32)])
    return {
        'x': nrm(ks[0], (BATCH, SEQ, d), 1.0),
        'c': nrm(ks[1], (BATCH, d), 1.0),
        'ada_w': nrm(ks[2], (DEPTH, d, 6 * d), 0.5 * d ** -0.5),
        'ada_b': nrm(ks[3], (DEPTH, 6 * d), 0.01),
        'norm_mix_g': gain(ks[4], (DEPTH, d)),
        'norm_ffn_g': gain(ks[5], (DEPTH, d)),
        'ma_w_in': nrm(ks[6], (N_A_LAYERS, d, MA_PROJ), d ** -0.5),
        'ma_conv_w': nrm(ks[7], (N_A_LAYERS, MA_CONV, MA_QK_COLS), MA_CONV ** -0.5),
        'ma_b_if': if_base + nrm(ks[8], (N_A_LAYERS, 2 * MA_HEADS), 0.1),
        'ma_hnorm_g': gain(ks[9], (N_A_LAYERS, MA_HEADS, MA_V)),
        'ma_w_out': nrm(ks[10], (N_A_LAYERS, d, d), d ** -0.5),
        'kv_ada_w': nrm(ks[11], (d, 2 * d), 0.5 * d ** -0.5),
        'kv_ada_b': nrm(ks[12], (2 * d,), 0.01),
        'kv_norm_g': gain(ks[13], (d,)),
        'kv_w': nrm(ks[14], (d, 2 * d), d ** -0.5),
        'k_norm_g': gain(ks[15], (SB_DIM,)),
        'sb_w_q': nrm(ks[16], (N_B_LAYERS, d, d), d ** -0.5),
        'sb_q_norm_g': gain(ks[17], (N_B_LAYERS, SB_DIM)),
        'sb_w_out': nrm(ks[18], (N_B_LAYERS, d, d), d ** -0.5),
        'peer_w_q': nrm(ks[19], (DEPTH, d, PEER_HEADS * PEER_QDIM), d ** -0.5),
        'peer_sub_keys': nrm(ks[20], (DEPTH, 2, PEER_NKEYS, PEER_QDIM // 2), (PEER_QDIM // 2) ** -0.5),
        'peer_u': nrm(ks[21], (DEPTH, PEER_EXPERTS, d), d ** -0.5),
        'peer_v': nrm(ks[22], (DEPTH, PEER_EXPERTS, d), PEER_HEADS ** -0.5),
    }


def reference(x, c, ada_w, ada_b, norm_mix_g, norm_ffn_g, ma_w_in, ma_conv_w, ma_b_if, ma_hnorm_g, ma_w_out,
              kv_ada_w, kv_ada_b, kv_norm_g, kv_w, k_norm_g, sb_w_q, sb_q_norm_g, sb_w_out,
              peer_w_q, peer_sub_keys, peer_u, peer_v):
    k_sh = None
    v_sh = None
    for layer in range(DEPTH):
        sh1, sc1, g1, sh2, sc2, g2 = jnp.split(ada_mod(c, ada_w[layer], ada_b[layer]), 6, axis=-1)
        h = modulate(rmsnorm(x, norm_mix_g[layer]), sh1, sc1)
        if layer < N_A_LAYERS:
            y = mlstm_mixer(h, ma_w_in[layer], ma_conv_w[layer], ma_b_if[layer], ma_hnorm_g[layer], ma_w_out[layer])
        else:
            if layer == N_A_LAYERS:
                k_sh, v_sh = shared_kv(x, c, kv_ada_w, kv_ada_b, kv_norm_g, kv_w, k_norm_g)
            j = layer - N_A_LAYERS
            y = stick_breaking_mixer(h, k_sh, v_sh, sb_w_q[j], sb_q_norm_g[j], sb_w_out[j])
        x = x + g1[:, None, :] * y
        h = modulate(rmsnorm(x, norm_ffn_g[layer]), sh2, sc2)
        x = x + g2[:, None, :] * peer_ffn(h, peer_w_q[layer], peer_sub_keys[layer], peer_u[layer], peer_v[layer])
    return x
```

```python
import functools

import jax
import jax.numpy as jnp
from jax import lax
from jax.experimental import pallas as pl
from jax.experimental.pallas import tpu as pltpu

F32 = jnp.float32
BF16 = jnp.bfloat16
I32 = jnp.int32

EPS = 1e-6
PEER_TOPK = 16
LANES = 128
SUBLANES = 8
VMEM_LIMIT_BYTES = 56 * 1024 * 1024

MLSTM_CHUNK = 256
SB_TILE = 128
MM_TILE_M = 512
ROUTE_TILE = 256
ROUTE_SUB = 128
PEER_TILE = 64
BITREV8 = (0, 4, 2, 6, 1, 5, 3, 7)


def _cparams(sem):
    return pltpu.CompilerParams(dimension_semantics=sem, vmem_limit_bytes=VMEM_LIMIT_BYTES)


def _dot(a, b):
    return jnp.dot(a, b, preferred_element_type=F32)


def _dot_tb(a, b):
    return lax.dot_general(a, b, (((1,), (1,)), ((), ())), preferred_element_type=F32)


def _split_dot(x, m):
    hi = x.astype(BF16)
    lo = (x - hi.astype(F32)).astype(BF16)
    return _dot(hi, m) + _dot(lo, m)


def _sigmoid(x):
    return 1.0 / (1.0 + jnp.exp(-x))


def _softplus(x):
    return jnp.maximum(x, 0.0) + jnp.log1p(jnp.exp(-jnp.abs(x)))


def _rmsnorm_mod(x, g, shift, scale):
    ms = jnp.mean(x * x, axis=-1, keepdims=True)
    y = x * lax.rsqrt(ms + EPS) * g
    return y * (1.0 + scale) + shift


def _ada_body(c_ref, w_ref, b_ref, o_ref):
    c = c_ref[...]
    a = (c * _sigmoid(c)).astype(BF16)
    o_ref[...] = _dot(a, w_ref[...].astype(BF16)) + b_ref[...]


def _ada(c, w, b):
    nl, d, e = w.shape
    bsz = c.shape[0]
    te = 1024
    return pl.pallas_call(
        _ada_body,
        out_shape=jax.ShapeDtypeStruct((nl, bsz, e), F32),
        grid=(nl, e // te),
        in_specs=[
            pl.BlockSpec((bsz, d), lambda l, j: (0, 0)),
            pl.BlockSpec((None, d, te), lambda l, j: (l, 0, j)),
            pl.BlockSpec((None, 1, te), lambda l, j: (l, 0, j)),
        ],
        out_specs=pl.BlockSpec((None, bsz, te), lambda l, j: (l, 0, j)),
        compiler_params=_cparams(("arbitrary", "arbitrary")),
        name="ada_mod",
    )(c, w, b.reshape(nl, 1, e))


def _nmm_body(*refs, prologue, epilogue, head_dim):
    it = iter(refs)
    x_ref = next(it)
    if prologue:
        g_ref, sh_ref, sc_ref = next(it), next(it), next(it)
    w_ref = next(it)
    if epilogue == "resid":
        res_ref, gate_ref = next(it), next(it)
    if epilogue == "headnorm":
        hg_ref = next(it)
    o_ref = next(it)
    h_ref = next(it)

    @pl.when(pl.program_id(1) == 0)
    def _():
        x = x_ref[...]
        if prologue:
            x = _rmsnorm_mod(x, g_ref[...], sh_ref[...], sc_ref[...])
        h_ref[...] = x.astype(BF16)

    acc = _dot(h_ref[...], w_ref[...])
    if epilogue == "resid":
        acc = res_ref[...] + gate_ref[...] * acc
    elif epilogue == "headnorm":
        r = lax.broadcasted_iota(I32, (LANES, LANES), 0) // head_dim
        c = lax.broadcasted_iota(I32, (LANES, LANES), 1) // head_dim
        group = jnp.where(r == c, 1.0, 0.0).astype(BF16)
        sq = acc * acc
        ms = jnp.concatenate(
            [_split_dot(sq[:, j * LANES:(j + 1) * LANES], group) for j in range(acc.shape[1] // LANES)],
            axis=1) * (1.0 / head_dim)
        acc = acc * lax.rsqrt(ms + EPS) * hg_ref[...]
    o_ref[...] = acc.astype(o_ref.dtype)


def _nmm(x, w, *, seq, norm=None, resid=None, headnorm=None, out_dtype=F32, tn=None, name):
    n, d = x.shape
    e = w.shape[1]
    tm = min(MM_TILE_M, seq)
    tn = e if tn is None else tn
    tpb = seq // tm
    args = [x]
    in_specs = [pl.BlockSpec((tm, d), lambda i, j: (i, 0))]
    if norm is not None:
        gain, mod, i_sh, i_sc = norm
        args += [gain, mod, mod]
        in_specs += [
            pl.BlockSpec((1, d), lambda i, j: (0, 0)),
            pl.BlockSpec((None, None, 1, d), lambda i, j: (i // tpb, i_sh, 0, 0)),
            pl.BlockSpec((None, None, 1, d), lambda i, j: (i // tpb, i_sc, 0, 0)),
        ]
    args.append(w)
    in_specs.append(pl.BlockSpec((d, tn), lambda i, j: (0, j)))
    epilogue = None
    head_dim = 0
    if resid is not None:
        res, mod, i_g = resid
        epilogue = "resid"
        args += [res, mod]
        in_specs += [
            pl.BlockSpec((tm, tn), lambda i, j: (i, j)),
            pl.BlockSpec((None, None, 1, tn), lambda i, j: (i // tpb, i_g, 0, j)),
        ]
    if headnorm is not None:
        hg, head_dim = headnorm
        epilogue = "headnorm"
        args.append(hg)
        in_specs.append(pl.BlockSpec((1, tn), lambda i, j: (0, j)))
    return pl.pallas_call(
        functools.partial(_nmm_body, prologue=norm is not None, epilogue=epilogue, head_dim=head_dim),
        out_shape=jax.ShapeDtypeStruct((n, e), out_dtype),
        grid=(n // tm, e // tn),
        in_specs=in_specs,
        out_specs=pl.BlockSpec((tm, tn), lambda i, j: (i, j)),
        scratch_shapes=[pltpu.VMEM((tm, d), BF16)],
        compiler_params=_cparams(("arbitrary", "arbitrary")),
        name=name,
    )(*args)


def _mlstm_body(qk_ref, v_ref, o_ref, gt_ref, cw_ref, bif_ref, hg_ref, out_ref,
                tail_ref, c_ref, m_ref, *, heads, qk_dim, conv_w):
    L = qk_ref.shape[0]
    nqk = heads * qk_dim
    vdim = LANES

    @pl.when(pl.program_id(1) == 0)
    def _():
        tail_ref[...] = jnp.zeros_like(tail_ref)
        c_ref[...] = jnp.zeros_like(c_ref)
        m_ref[...] = jnp.zeros_like(m_ref)

    x = qk_ref[...]
    tail = tail_ref[...]
    rows8 = lax.broadcasted_iota(I32, (SUBLANES, x.shape[1]), 0)
    acc = x * cw_ref[conv_w - 1:conv_w, :]
    for j in range(1, conv_w):
        xs = pltpu.roll(x, j, axis=0)
        head8 = jnp.where(rows8 < j, pltpu.roll(tail, j, axis=0), xs[0:SUBLANES])
        xs = jnp.concatenate([head8, xs[SUBLANES:]], axis=0)
        acc = acc + xs * cw_ref[conv_w - 1 - j:conv_w - j, :]
    tail_ref[...] = x[L - SUBLANES:, :]
    qk = acc * _sigmoid(acc)
    q_all = qk[:, :nqk] * (qk_dim ** -0.5)
    k_all = qk[:, nqk:]
    kt_all = k_all.T.astype(BF16)

    gates = gt_ref[...] + bif_ref[...]
    log_f = -_softplus(-gates)
    r = lax.broadcasted_iota(I32, (L, L), 0)
    c = lax.broadcasted_iota(I32, (L, L), 1)
    causal = r >= c
    tri = jnp.where(causal, 1.0, 0.0).astype(BF16)
    lf_hi = log_f.astype(BF16)
    lf_lo = (log_f - lf_hi.astype(F32)).astype(BF16)
    bcum = _dot(tri, lf_hi) + _dot(tri, lf_lo)
    gates_t = gates.T
    bcum_t = bcum.T
    lane = lax.broadcasted_iota(I32, (L, LANES), 1)
    ones_v = jnp.ones((L, vdim), BF16)
    rows_c = lax.broadcasted_iota(I32, (LANES, 2 * vdim), 0)

    for h in range(heads):
        pair, half = h // 2, h % 2
        b_col = bcum[:, heads + h:heads + h + 1]
        i_col = gates[:, h:h + 1]
        b_row = bcum_t[heads + h:heads + h + 1, :]
        i_row = gates_t[h:h + 1, :]
        b_last = b_col[L - 1:L, :]
        m_prev = m_ref[h:h + 1, 0:1]
        in_head = (lane >= half * qk_dim) & (lane < (half + 1) * qk_dim)
        q_h = jnp.where(in_head, q_all[:, pair * LANES:(pair + 1) * LANES], 0.0).astype(BF16)
        kt_h = kt_all[pair * LANES:(pair + 1) * LANES, :]
        v_h = v_ref[:, h * vdim:(h + 1) * vdim]
        v_aug = jnp.concatenate([v_h.astype(BF16), ones_v], axis=1)

        log_d = jnp.where(causal, b_col - b_row + i_row, -jnp.inf)
        inter_log = b_col + m_prev
        m_t = jnp.maximum(inter_log, jnp.max(log_d, axis=1, keepdims=True))
        inter_w = jnp.exp(inter_log - m_t)
        s = _dot(q_h, kt_h) * jnp.exp(log_d - m_t)
        c_prev = c_ref[h]
        tot = _dot(s.astype(BF16), v_aug) + inter_w * _dot(q_h, c_prev.astype(BF16))
        num = tot[:, :vdim]
        den = tot[:, vdim:]
        hh = num / jnp.maximum(jnp.abs(den), jnp.exp(-m_t))
        ms = jnp.mean(hh * hh, axis=1, keepdims=True)
        hh = hh * lax.rsqrt(ms + EPS) * hg_ref[:, h * vdim:(h + 1) * vdim]
        hh = hh * _sigmoid(o_ref[:, h * vdim:(h + 1) * vdim])
        out_ref[:, h * vdim:(h + 1) * vdim] = hh.astype(out_ref.dtype)

        w_state = b_last - b_col + i_col
        m_loc = jnp.max(w_state, axis=0, keepdims=True)
        e_state = jnp.exp(w_state - m_loc)
        ev = (e_state * jnp.concatenate([v_h, jnp.ones((L, vdim), F32)], axis=1)).astype(BF16)
        c_loc = _dot(kt_h, ev)
        in_rows = (rows_c >= half * qk_dim) & (rows_c < (half + 1) * qk_dim)
        c_loc = jnp.where(in_rows, c_loc, 0.0)
        m_new = jnp.maximum(b_last + m_prev, m_loc)
        a = jnp.exp(b_last + m_prev - m_new)
        rr = jnp.exp(m_loc - m_new)
        c_ref[h] = a * c_prev + rr * c_loc
        m_ref[h:h + 1, :] = jnp.broadcast_to(m_new, (1, LANES))


def _mlstm(proj, conv_w, b_if, hnorm_g, *, bsz, seq, d):
    heads, vdim = hnorm_g.shape
    assert vdim == LANES
    nqk2 = conv_w.shape[1]
    qk_dim = nqk2 // (2 * heads)
    width = conv_w.shape[0]
    L = min(MLSTM_CHUNK, seq)
    nc = seq // L
    bif = jnp.zeros((1, LANES), F32).at[0, :2 * heads].set(b_if)
    gate_blk = (nqk2 + 2 * d) // LANES
    return pl.pallas_call(
        functools.partial(_mlstm_body, heads=heads, qk_dim=qk_dim, conv_w=width),
        out_shape=jax.ShapeDtypeStruct((bsz * seq, d), BF16),
        grid=(bsz, nc),
        in_specs=[
            pl.BlockSpec((L, nqk2), lambda b, c: (b * nc + c, 0)),
            pl.BlockSpec((L, d), lambda b, c: (b * nc + c, nqk2 // d)),
            pl.BlockSpec((L, d), lambda b, c: (b * nc + c, nqk2 // d + 1)),
            pl.BlockSpec((L, LANES), lambda b, c: (b * nc + c, gate_blk)),
            pl.BlockSpec((width, nqk2), lambda b, c: (0, 0)),
            pl.BlockSpec((1, LANES), lambda b, c: (0, 0)),
            pl.BlockSpec((1, d), lambda b, c: (0, 0)),
        ],
        out_specs=pl.BlockSpec((L, d), lambda b, c: (b * nc + c, 0)),
        scratch_shapes=[
            pltpu.VMEM((SUBLANES, nqk2), F32),
            pltpu.VMEM((heads, LANES, 2 * LANES), F32),
            pltpu.VMEM((heads, LANES), F32),
        ],
        compiler_params=_cparams(("arbitrary", "arbitrary")),
        name="mlstm",
    )(proj, proj, proj, proj, conv_w, bif, hnorm_g.reshape(1, d))


def _sb_body(q_ref, k_ref, v_ref, o_ref, *, head_dim):
    T = q_ref.shape[0]
    qi = pl.program_id(2)
    q = q_ref[...].astype(F32)
    lane = lax.broadcasted_iota(I32, (T, LANES), 1)
    q_heads = [jnp.where(lane < head_dim, q, 0.0).astype(BF16), jnp.where(lane >= head_dim, q, 0.0).astype(BF16)]
    row = lax.broadcasted_iota(I32, (T, T), 0)
    col = lax.broadcasted_iota(I32, (T, T), 1)
    rr = lax.broadcasted_iota(I32, (T, 2 * T), 0)
    cc = lax.broadcasted_iota(I32, (T, 2 * T), 1)
    m_aug = jnp.where((cc >= T) | (rr > cc), 1.0, 0.0).astype(BF16)

    def step(j, carry):
        kb = qi - j
        off = pl.multiple_of(kb * T, T)
        kblk = k_ref[pl.ds(off, T), :]
        vblk = v_ref[pl.ds(off, T), :]
        strict = (col + kb * T) < (row + qi * T)
        new = []
        for hd in range(2):
            o_acc, run = carry[2 * hd], carry[2 * hd + 1]
            z = _dot_tb(q_heads[hd], kblk)
            sp = _softplus(z)
            l = jnp.where(strict, sp, 0.0)
            res = _split_dot(l, m_aug)
            a = jnp.where(strict, jnp.exp(z - sp - res[:, :T] - run), 0.0)
            new += [o_acc + _dot(a.astype(BF16), vblk), run + res[:, T:]]
        return tuple(new)

    z0 = jnp.zeros((T, LANES), F32)
    z1 = jnp.zeros((T, T), F32)
    o_a, _, o_b, _ = lax.fori_loop(0, qi + 1, step, (z0, z1, z0, z1))
    o_ref[...] = jnp.where(lane < head_dim, o_a, o_b).astype(o_ref.dtype)


def _sb_attention(q, k, v, *, bsz, seq, head_dim):
    n, d = q.shape
    assert 2 * head_dim == LANES
    t = min(SB_TILE, seq)
    q3, k3, v3 = (a.reshape(bsz, seq, d) for a in (q, k, v))
    out = pl.pallas_call(
        functools.partial(_sb_body, head_dim=head_dim),
        out_shape=jax.ShapeDtypeStruct((bsz, seq, d), BF16),
        grid=(bsz, d // LANES, seq // t),
        in_specs=[
            pl.BlockSpec((None, t, LANES), lambda b, hp, i: (b, i, hp)),
            pl.BlockSpec((None, seq, LANES), lambda b, hp, i: (b, 0, hp)),
            pl.BlockSpec((None, seq, LANES), lambda b, hp, i: (b, 0, hp)),
        ],
        out_specs=pl.BlockSpec((None, t, LANES), lambda b, hp, i: (b, i, hp)),
        compiler_params=_cparams(("arbitrary", "arbitrary", "arbitrary")),
        name="sb_attention",
    )(q3, k3, v3)
    return out.reshape(n, d)


def _topk_rows(s, k):
    rows = lax.broadcasted_iota(I32, s.shape, 0).astype(F32)
    big = float(s.shape[0])
    vals, idxs = [], []
    for _ in range(k):
        m = jnp.max(s, axis=0, keepdims=True)
        i = jnp.min(jnp.where(s == m, rows, big), axis=0, keepdims=True)
        vals.append(m)
        idxs.append(i)
        s = jnp.where(rows == i, -jnp.inf, s)
    return jnp.concatenate(vals, axis=0), jnp.concatenate(idxs, axis=0)


def _select_rows(table, sel, k):
    out = jnp.zeros_like(sel)
    for a in range(k):
        out = jnp.where(sel == float(a), table[a:a + 1, :], out)
    return out


def _route_body(x_ref, g_ref, sh_ref, sc_ref, wq_ref, keys_ref, h_ref, idx_ref, gate_ref,
                q_scr, idx_t, gate_t, *, heads, nkeys):
    k = PEER_TOPK
    tm = x_ref.shape[0]
    hmod = _rmsnorm_mod(x_ref[...], g_ref[...], sh_ref[...], sc_ref[...])
    h_ref[...] = hmod
    q_scr[...] = _dot(hmod.astype(BF16), wq_ref[...]).astype(BF16)
    nsub = tm // ROUTE_SUB

    def head_step(it, carry):
        hd = it // nsub
        sub = it % nsub
        tok0 = pl.multiple_of(sub * ROUTE_SUB, ROUTE_SUB)
        tops = []
        for p in range(2):
            c0 = pl.multiple_of((hd * 2 + p) * LANES, LANES)
            qs = q_scr[pl.ds(tok0, ROUTE_SUB), pl.ds(c0, LANES)]
            st = _dot_tb(keys_ref[p], qs)
            tops.append(_topk_rows(st, k))
        (v0, i0), (v1, i1) = tops
        cand = jnp.concatenate([v0[a:a + 1, :] + v1 for a in range(k)], axis=0)
        g_top, pos = _topk_rows(cand, k)
        pa = jnp.floor(pos * (1.0 / k))
        pb = pos - pa * k
        expert = _select_rows(i0, pa, k) * float(nkeys) + _select_rows(i1, pb, k)
        e = jnp.exp(g_top - g_top[0:1, :])
        gate = e / jnp.sum(e, axis=0, keepdims=True)
        r0 = pl.multiple_of(hd * k, k)
        idx_t[pl.ds(r0, k), pl.ds(tok0, ROUTE_SUB)] = expert
        gate_t[pl.ds(r0, k), pl.ds(tok0, ROUTE_SUB)] = gate
        return carry

    lax.fori_loop(0, heads * nsub, head_step, 0)
    idx_ref[...] = idx_t[...].T.astype(I32)
    gate_ref[...] = gate_t[...].T


def _peer_route(x, gain, mod, i_sh, i_sc, wq, keys, *, seq):
    n, d = x.shape
    qcols = wq.shape[1]
    nkeys, kdim = keys.shape[1], keys.shape[2]
    assert kdim == LANES and nkeys == LANES
    heads = qcols // (2 * kdim)
    tm = min(ROUTE_TILE, seq)
    tpb = seq // tm
    ha = heads * PEER_TOPK
    return pl.pallas_call(
        functools.partial(_route_body, heads=heads, nkeys=nkeys),
        out_shape=(jax.ShapeDtypeStruct((n, d), F32),
                   jax.ShapeDtypeStruct((n, ha), I32),
                   jax.ShapeDtypeStruct((n, ha), F32)),
        grid=(n // tm,),
        in_specs=[
            pl.BlockSpec((tm, d), lambda i: (i, 0)),
            pl.BlockSpec((1, d), lambda i: (0, 0)),
            pl.BlockSpec((None, None, 1, d), lambda i: (i // tpb, i_sh, 0, 0)),
            pl.BlockSpec((None, None, 1, d), lambda i: (i // tpb, i_sc, 0, 0)),
            pl.BlockSpec((d, qcols), lambda i: (0, 0)),
            pl.BlockSpec((2, nkeys, kdim), lambda i: (0, 0, 0)),
        ],
        out_specs=(pl.BlockSpec((tm, d), lambda i: (i, 0)),
                   pl.BlockSpec((tm, ha), lambda i: (i, 0)),
                   pl.BlockSpec((tm, ha), lambda i: (i, 0))),
        scratch_shapes=[pltpu.VMEM((tm, qcols), BF16),
                        pltpu.VMEM((ha, tm), F32),
                        pltpu.VMEM((ha, tm), F32)],
        compiler_params=_cparams(("arbitrary",)),
        name="peer_route",
    )(x, gain, mod, mod, wq, keys)


def _pack_table(t):
    e, d = t.shape
    assert d == SUBLANES * LANES
    half = SUBLANES // 2
    bits = lax.bitcast_convert_type(t.astype(BF16).reshape(e, 2, half, LANES), jnp.uint16).astype(jnp.uint32)
    return lax.bitcast_convert_type(bits[:, 0] | (bits[:, 1] << 16), I32)


def _unpack(w):
    lo = lax.bitcast_convert_type(w << 16, F32)
    hi = lax.bitcast_convert_type(w & jnp.int32(-65536), F32)
    return lo, hi


def _merge(v1, v2, g, first_half):
    h = g // 2
    a = v1 + pltpu.roll(v1, SUBLANES - h, axis=0)
    b = v2 + pltpu.roll(v2, h, axis=0)
    return jnp.where(first_half, a, b)


def _erf(x):
    ax = jnp.abs(x)
    t = 1.0 / (1.0 + 0.3275911 * ax)
    poly = ((((1.061405429 * t - 1.453152027) * t + 1.421413741) * t - 0.284496736) * t + 0.254829592) * t
    y = 1.0 - poly * jnp.exp(-ax * ax)
    return jnp.where(x < 0, -y, y)


def _peer_u_body(idx_ref, tbl_ref, x_ref, g_ref, w_ref, *, use_lax_erf):
    tt, na = g_ref.shape
    half = SUBLANES // 2
    sub = lax.broadcasted_iota(I32, (SUBLANES, LANES), 0)
    first4 = (sub & 3) < 2
    first2 = (sub & 1) < 1
    ones8 = jnp.ones((SUBLANES, LANES), BF16)

    def tok(t, carry):
        xw = x_ref[t]
        xl, xh = xw[0:half], xw[half:SUBLANES]
        groups = []
        for grp in range(na // SUBLANES):
            ps = []
            for kk in range(SUBLANES):
                lo, hi = _unpack(tbl_ref[idx_ref[t, grp * SUBLANES + BITREV8[kk]]])
                ps.append(lo * xl + hi * xh)
            c = [jnp.concatenate([ps[2 * i], ps[2 * i + 1]], axis=0) for i in range(4)]
            d0 = _merge(c[0], c[1], 4, first4)
            d1 = _merge(c[2], c[3], 4, first4)
            groups.append(_merge(d0, d1, 2, first2))
        part = jnp.concatenate(groups, axis=0)
        hi_p = part.astype(BF16)
        lo_p = (part - hi_p.astype(F32)).astype(BF16)
        act = (_dot_tb(ones8, hi_p) + _dot_tb(ones8, lo_p))[0:1, :]
        if use_lax_erf:
            gelu = 0.5 * act * (1.0 + lax.erf(act * (2.0 ** -0.5)))
        else:
            gelu = 0.5 * act * (1.0 + _erf(act * (2.0 ** -0.5)))
        w_ref[pl.ds(t, 1), :] = g_ref[pl.ds(t, 1), :] * gelu
        return carry

    lax.fori_loop(0, tt, tok, 0)


def _peer_v_body(idx_ref, w_ref, tbl_ref, res_ref, gate_ref, o_ref):
    tt = res_ref.shape[0]
    na = idx_ref.shape[1]
    half = SUBLANES // 2
    nacc = 4

    def tok(t, carry):
        acc_lo = [jnp.zeros((half, LANES), F32) for _ in range(nacc)]
        acc_hi = [jnp.zeros((half, LANES), F32) for _ in range(nacc)]
        for a in range(na):
            lo, hi = _unpack(tbl_ref[idx_ref[t, a]])
            s = w_ref[t, a]
            acc_lo[a % nacc] = acc_lo[a % nacc] + s * lo
            acc_hi[a % nacc] = acc_hi[a % nacc] + s * hi
        y = jnp.concatenate([(acc_lo[0] + acc_lo[1]) + (acc_lo[2] + acc_lo[3]),
                             (acc_hi[0] + acc_hi[1]) + (acc_hi[2] + acc_hi[3])], axis=0)
        o_ref[t] = res_ref[t] + gate_ref[...] * y
        return carry

    lax.fori_loop(0, tt, tok, 0)


def _peer_experts(x, hmod, idx, gate, tbl_u, tbl_v, mod8, i_gate, *, seq):
    n, d = x.shape
    na = idx.shape[1]
    tt = min(PEER_TILE, seq)
    tpb = seq // tt
    x8 = x.reshape(n, SUBLANES, LANES)
    h8 = hmod.reshape(n, SUBLANES, LANES)
    smem = functools.partial(pl.BlockSpec, memory_space=pltpu.SMEM)
    tbl_spec = pl.BlockSpec(memory_space=pltpu.VMEM)
    w = pl.pallas_call(
        functools.partial(_peer_u_body, use_lax_erf=False),
        out_shape=jax.ShapeDtypeStruct((n, na), F32),
        grid=(n // tt,),
        in_specs=[
            smem((tt, na), lambda i: (i, 0)),
            tbl_spec,
            pl.BlockSpec((tt, SUBLANES, LANES), lambda i: (i, 0, 0)),
            pl.BlockSpec((tt, na), lambda i: (i, 0)),
        ],
        out_specs=pl.BlockSpec((tt, na), lambda i: (i, 0)),
        compiler_params=_cparams(("arbitrary",)),
        name="peer_u",
    )(idx, tbl_u, h8, gate)
    out = pl.pallas_call(
        _peer_v_body,
        out_shape=jax.ShapeDtypeStruct((n, SUBLANES, LANES), F32),
        grid=(n // tt,),
        in_specs=[
            smem((tt, na), lambda i: (i, 0)),
            smem((tt, na), lambda i: (i, 0)),
            tbl_spec,
            pl.BlockSpec((tt, SUBLANES, LANES), lambda i: (i, 0, 0)),
            pl.BlockSpec((None, None, SUBLANES, LANES), lambda i: (i // tpb, i_gate, 0, 0)),
        ],
        out_specs=pl.BlockSpec((tt, SUBLANES, LANES), lambda i: (i, 0, 0)),
        compiler_params=_cparams(("arbitrary",)),
        name="peer_v",
    )(idx, w, tbl_v, x8, mod8)
    return out.reshape(n, d)


def _peer_ffn(x, gain, mod, wq, sub_keys, u, v, *, seq):
    bsz = mod.shape[0]
    d = x.shape[1]
    hmod, idx, gate = _peer_route(x, gain, mod, 3, 4, wq.astype(BF16), sub_keys.astype(BF16), seq=seq)
    mod8 = mod.reshape(bsz, mod.shape[1], SUBLANES, LANES)
    return _peer_experts(x, hmod, idx, gate, _pack_table(u), _pack_table(v), mod8, 5, seq=seq)


def kernel(x, c, ada_w, ada_b, norm_mix_g, norm_ffn_g, ma_w_in, ma_conv_w, ma_b_if, ma_hnorm_g, ma_w_out,
           kv_ada_w, kv_ada_b, kv_norm_g, kv_w, k_norm_g, sb_w_q, sb_q_norm_g, sb_w_out,
           peer_w_q, peer_sub_keys, peer_u, peer_v):
    bsz, seq, d = x.shape
    n = bsz * seq
    depth = ada_w.shape[0]
    n_a = ma_w_in.shape[0]
    sb_dim = k_norm_g.shape[0]
    sb_heads = d // sb_dim

    mods = _ada(c, ada_w, ada_b).reshape(depth, bsz, 6, 1, d)
    kv_mod = _ada(c, kv_ada_w[None], kv_ada_b[None]).reshape(bsz, 2, 1, d)
    xs = x.reshape(n, d)
    k_sh = v_sh = None
    for layer in range(depth):
        mod = mods[layer]
        g_mix = norm_mix_g[layer].reshape(1, d)
        if layer < n_a:
            w_in = ma_w_in[layer]
            pad = (-w_in.shape[1]) % LANES
            w_in = jnp.pad(w_in, ((0, 0), (0, pad))).astype(BF16)
            ncols = w_in.shape[1]
            tn = ncols // 5 if ncols % (5 * LANES) == 0 else None
            proj = _nmm(xs, w_in, seq=seq, norm=(g_mix, mod, 0, 1), tn=tn, name="mlstm_proj")
            hh = _mlstm(proj, ma_conv_w[layer], ma_b_if[layer], ma_hnorm_g[layer], bsz=bsz, seq=seq, d=d)
            xs = _nmm(hh, ma_w_out[layer].astype(BF16), seq=seq, resid=(xs, mod, 2), name="mlstm_out")
        else:
            if layer == n_a:
                kvw = kv_w.astype(BF16)
                k_gain = jnp.tile(k_norm_g, sb_heads).reshape(1, d)
                g_kv = kv_norm_g.reshape(1, d)
                k_sh = _nmm(xs, kvw[:, :d], seq=seq, norm=(g_kv, kv_mod, 0, 1),
                            headnorm=(k_gain, sb_dim), out_dtype=BF16, name="kv_k")
                v_sh = _nmm(xs, kvw[:, d:], seq=seq, norm=(g_kv, kv_mod, 0, 1), out_dtype=BF16, name="kv_v")
            j = layer - n_a
            q_gain = (jnp.tile(sb_q_norm_g[j], sb_heads) * (sb_dim ** -0.5)).reshape(1, d)
            q = _nmm(xs, sb_w_q[j].astype(BF16), seq=seq, norm=(g_mix, mod, 0, 1),
                     headnorm=(q_gain, sb_dim), out_dtype=BF16, name="sb_q")
            o = _sb_attention(q, k_sh, v_sh, bsz=bsz, seq=seq, head_dim=sb_dim)
            xs = _nmm(o, sb_w_out[j].astype(BF16), seq=seq, resid=(xs, mod, 2), name="sb_out")
        xs = _peer_ffn(xs, norm_ffn_g[layer].reshape(1, d), mod, peer_w_q[layer], peer_sub_keys[layer],
                       peer_u[layer], peer_v[layer], seq=seq)
    return xs.reshape(bsz, seq, d)
```

```python
import functools

import jax
import jax.numpy as jnp
from jax import lax
from jax.experimental import pallas as pl
from jax.experimental.pallas import tpu as pltpu

F32 = jnp.float32
BF16 = jnp.bfloat16
I32 = jnp.int32

EPS = 1e-6
PEER_TOPK = 16
LANES = 128
SUBLANES = 8
VMEM_LIMIT_BYTES = 56 * 1024 * 1024

MLSTM_CHUNK = 256
SB_TILE_Q = 256
SB_TILE_K = 128
SB_LANES = 512
LOG2E = 1.4426950408889634
MM_TILE_M = 512
ROUTE_TILE = 256
ROUTE_SUB = 128
PEER_TILE = 64
BITREV8 = (0, 4, 2, 6, 1, 5, 3, 7)


def _cparams(sem):
    return pltpu.CompilerParams(dimension_semantics=sem, vmem_limit_bytes=VMEM_LIMIT_BYTES)


def _dot(a, b):
    return jnp.dot(a, b, preferred_element_type=F32)


def _dot_tb(a, b):
    return lax.dot_general(a, b, (((1,), (1,)), ((), ())), preferred_element_type=F32)


def _split_dot(x, m):
    hi = x.astype(BF16)
    lo = (x - hi.astype(F32)).astype(BF16)
    return _dot(hi, m) + _dot(lo, m)


def _sigmoid(x):
    return 1.0 / (1.0 + jnp.exp(-x))


def _softplus(x):
    return jnp.maximum(x, 0.0) + jnp.log1p(jnp.exp(-jnp.abs(x)))


def _rmsnorm_mod(x, g, shift, scale):
    ms = jnp.mean(x * x, axis=-1, keepdims=True)
    y = x * lax.rsqrt(ms + EPS) * g
    return y * (1.0 + scale) + shift


def _ada_body(c_ref, w_ref, b_ref, o_ref):
    c = c_ref[...]
    a = (c * _sigmoid(c)).astype(BF16)
    o_ref[...] = _dot(a, w_ref[...].astype(BF16)) + b_ref[...]


def _ada(c, w, b):
    nl, d, e = w.shape
    bsz = c.shape[0]
    te = 1024
    return pl.pallas_call(
        _ada_body,
        out_shape=jax.ShapeDtypeStruct((nl, bsz, e), F32),
        grid=(nl, e // te),
        in_specs=[
            pl.BlockSpec((bsz, d), lambda l, j: (0, 0)),
            pl.BlockSpec((None, d, te), lambda l, j: (l, 0, j)),
            pl.BlockSpec((None, 1, te), lambda l, j: (l, 0, j)),
        ],
        out_specs=pl.BlockSpec((None, bsz, te), lambda l, j: (l, 0, j)),
        compiler_params=_cparams(("arbitrary", "arbitrary")),
        name="ada_mod",
    )(c, w, b.reshape(nl, 1, e))


def _nmm_body(*refs, prologue, epilogue, head_dim):
    it = iter(refs)
    x_ref = next(it)
    if prologue:
        g_ref, sh_ref, sc_ref = next(it), next(it), next(it)
    w_ref = next(it)
    if epilogue == "resid":
        res_ref, gate_ref = next(it), next(it)
    if epilogue == "headnorm":
        hg_ref = next(it)
    o_ref = next(it)
    h_ref = next(it)

    @pl.when(pl.program_id(1) == 0)
    def _():
        x = x_ref[...]
        if prologue:
            x = _rmsnorm_mod(x, g_ref[...], sh_ref[...], sc_ref[...])
        h_ref[...] = x.astype(BF16)

    acc = _dot(h_ref[...], w_ref[...])
    if epilogue == "resid":
        acc = res_ref[...] + gate_ref[...] * acc
    elif epilogue == "headnorm":
        r = lax.broadcasted_iota(I32, (LANES, LANES), 0) // head_dim
        c = lax.broadcasted_iota(I32, (LANES, LANES), 1) // head_dim
        group = jnp.where(r == c, 1.0, 0.0).astype(BF16)
        sq = acc * acc
        ms = jnp.concatenate(
            [_split_dot(sq[:, j * LANES:(j + 1) * LANES], group) for j in range(acc.shape[1] // LANES)],
            axis=1) * (1.0 / head_dim)
        acc = acc * lax.rsqrt(ms + EPS) * hg_ref[...]
    o_ref[...] = acc.astype(o_ref.dtype)


def _nmm(x, w, *, seq, norm=None, resid=None, headnorm=None, out_dtype=F32, tn=None, name):
    n, d = x.shape
    e = w.shape[1]
    tm = min(MM_TILE_M, seq)
    tn = e if tn is None else tn
    tpb = seq // tm
    args = [x]
    in_specs = [pl.BlockSpec((tm, d), lambda i, j: (i, 0))]
    if norm is not None:
        gain, mod, i_sh, i_sc = norm
        args += [gain, mod, mod]
        in_specs += [
            pl.BlockSpec((1, d), lambda i, j: (0, 0)),
            pl.BlockSpec((None, None, 1, d), lambda i, j: (i // tpb, i_sh, 0, 0)),
            pl.BlockSpec((None, None, 1, d), lambda i, j: (i // tpb, i_sc, 0, 0)),
        ]
    args.append(w)
    in_specs.append(pl.BlockSpec((d, tn), lambda i, j: (0, j)))
    epilogue = None
    head_dim = 0
    if resid is not None:
        res, mod, i_g = resid
        epilogue = "resid"
        args += [res, mod]
        in_specs += [
            pl.BlockSpec((tm, tn), lambda i, j: (i, j)),
            pl.BlockSpec((None, None, 1, tn), lambda i, j: (i // tpb, i_g, 0, j)),
        ]
    if headnorm is not None:
        hg, head_dim = headnorm
        epilogue = "headnorm"
        args.append(hg)
        in_specs.append(pl.BlockSpec((1, tn), lambda i, j: (0, j)))
    return pl.pallas_call(
        functools.partial(_nmm_body, prologue=norm is not None, epilogue=epilogue, head_dim=head_dim),
        out_shape=jax.ShapeDtypeStruct((n, e), out_dtype),
        grid=(n // tm, e // tn),
        in_specs=in_specs,
        out_specs=pl.BlockSpec((tm, tn), lambda i, j: (i, j)),
        scratch_shapes=[pltpu.VMEM((tm, d), BF16)],
        compiler_params=_cparams(("arbitrary", "arbitrary")),
        name=name,
    )(*args)


def _mlstm_body(qk_ref, v_ref, o_ref, gt_ref, cw_ref, bif_ref, hg_ref, out_ref,
                tail_ref, c_ref, m_ref, *, heads, qk_dim, conv_w):
    L = qk_ref.shape[0]
    nqk = heads * qk_dim
    vdim = LANES

    @pl.when(pl.program_id(1) == 0)
    def _():
        tail_ref[...] = jnp.zeros_like(tail_ref)
        c_ref[...] = jnp.zeros_like(c_ref)
        m_ref[...] = jnp.zeros_like(m_ref)

    x = qk_ref[...]
    tail = tail_ref[...]
    rows8 = lax.broadcasted_iota(I32, (SUBLANES, x.shape[1]), 0)
    acc = x * cw_ref[conv_w - 1:conv_w, :]
    for j in range(1, conv_w):
        xs = pltpu.roll(x, j, axis=0)
        head8 = jnp.where(rows8 < j, pltpu.roll(tail, j, axis=0), xs[0:SUBLANES])
        xs = jnp.concatenate([head8, xs[SUBLANES:]], axis=0)
        acc = acc + xs * cw_ref[conv_w - 1 - j:conv_w - j, :]
    tail_ref[...] = x[L - SUBLANES:, :]
    qk = acc * _sigmoid(acc)
    q_all = qk[:, :nqk] * (qk_dim ** -0.5)
    k_all = qk[:, nqk:]
    kt_all = k_all.T.astype(BF16)

    gates = gt_ref[...] + bif_ref[...]
    log_f = -_softplus(-gates)
    r = lax.broadcasted_iota(I32, (L, L), 0)
    c = lax.broadcasted_iota(I32, (L, L), 1)
    causal = r >= c
    tri = jnp.where(causal, 1.0, 0.0).astype(BF16)
    lf_hi = log_f.astype(BF16)
    lf_lo = (log_f - lf_hi.astype(F32)).astype(BF16)
    bcum = _dot(tri, lf_hi) + _dot(tri, lf_lo)
    gates_t = gates.T
    bcum_t = bcum.T
    lane = lax.broadcasted_iota(I32, (L, LANES), 1)
    ones_v = jnp.ones((L, vdim), BF16)
    rows_c = lax.broadcasted_iota(I32, (LANES, 2 * vdim), 0)

    for h in range(heads):
        pair, half = h // 2, h % 2
        b_col = bcum[:, heads + h:heads + h + 1]
        i_col = gates[:, h:h + 1]
        b_row = bcum_t[heads + h:heads + h + 1, :]
        i_row = gates_t[h:h + 1, :]
        b_last = b_col[L - 1:L, :]
        m_prev = m_ref[h:h + 1, 0:1]
        in_head = (lane >= half * qk_dim) & (lane < (half + 1) * qk_dim)
        q_h = jnp.where(in_head, q_all[:, pair * LANES:(pair + 1) * LANES], 0.0).astype(BF16)
        kt_h = kt_all[pair * LANES:(pair + 1) * LANES, :]
        v_h = v_ref[:, h * vdim:(h + 1) * vdim]
        v_aug = jnp.concatenate([v_h.astype(BF16), ones_v], axis=1)

        log_d = jnp.where(causal, b_col - b_row + i_row, -jnp.inf)
        inter_log = b_col + m_prev
        m_t = jnp.maximum(inter_log, jnp.max(log_d, axis=1, keepdims=True))
        inter_w = jnp.exp(inter_log - m_t)
        s = _dot(q_h, kt_h) * jnp.exp(log_d - m_t)
        c_prev = c_ref[h]
        tot = _dot(s.astype(BF16), v_aug) + inter_w * _dot(q_h, c_prev.astype(BF16))
        num = tot[:, :vdim]
        den = tot[:, vdim:]
        hh = num / jnp.maximum(jnp.abs(den), jnp.exp(-m_t))
        ms = jnp.mean(hh * hh, axis=1, keepdims=True)
        hh = hh * lax.rsqrt(ms + EPS) * hg_ref[:, h * vdim:(h + 1) * vdim]
        hh = hh * _sigmoid(o_ref[:, h * vdim:(h + 1) * vdim])
        out_ref[:, h * vdim:(h + 1) * vdim] = hh.astype(out_ref.dtype)

        w_state = b_last - b_col + i_col
        m_loc = jnp.max(w_state, axis=0, keepdims=True)
        e_state = jnp.exp(w_state - m_loc)
        ev = (e_state * jnp.concatenate([v_h, jnp.ones((L, vdim), F32)], axis=1)).astype(BF16)
        c_loc = _dot(kt_h, ev)
        in_rows = (rows_c >= half * qk_dim) & (rows_c < (half + 1) * qk_dim)
        c_loc = jnp.where(in_rows, c_loc, 0.0)
        m_new = jnp.maximum(b_last + m_prev, m_loc)
        a = jnp.exp(b_last + m_prev - m_new)
        rr = jnp.exp(m_loc - m_new)
        c_ref[h] = a * c_prev + rr * c_loc
        m_ref[h:h + 1, :] = jnp.broadcast_to(m_new, (1, LANES))


def _mlstm(proj, conv_w, b_if, hnorm_g, *, bsz, seq, d):
    heads, vdim = hnorm_g.shape
    assert vdim == LANES
    nqk2 = conv_w.shape[1]
    qk_dim = nqk2 // (2 * heads)
    width = conv_w.shape[0]
    L = min(MLSTM_CHUNK, seq)
    nc = seq // L
    bif = jnp.zeros((1, LANES), F32).at[0, :2 * heads].set(b_if)
    gate_blk = (nqk2 + 2 * d) // LANES
    return pl.pallas_call(
        functools.partial(_mlstm_body, heads=heads, qk_dim=qk_dim, conv_w=width),
        out_shape=jax.ShapeDtypeStruct((bsz * seq, d), BF16),
        grid=(bsz, nc),
        in_specs=[
            pl.BlockSpec((L, nqk2), lambda b, c: (b * nc + c, 0)),
            pl.BlockSpec((L, d), lambda b, c: (b * nc + c, nqk2 // d)),
            pl.BlockSpec((L, d), lambda b, c: (b * nc + c, nqk2 // d + 1)),
            pl.BlockSpec((L, LANES), lambda b, c: (b * nc + c, gate_blk)),
            pl.BlockSpec((width, nqk2), lambda b, c: (0, 0)),
            pl.BlockSpec((1, LANES), lambda b, c: (0, 0)),
            pl.BlockSpec((1, d), lambda b, c: (0, 0)),
        ],
        out_specs=pl.BlockSpec((L, d), lambda b, c: (b * nc + c, 0)),
        scratch_shapes=[
            pltpu.VMEM((SUBLANES, nqk2), F32),
            pltpu.VMEM((heads, LANES, 2 * LANES), F32),
            pltpu.VMEM((heads, LANES), F32),
        ],
        compiler_params=_cparams(("arbitrary", "arbitrary")),
        name="mlstm",
    )(proj, proj, proj, proj, conv_w, bif, hnorm_g.reshape(1, d))


def _sb_body(q_ref, k_ref, v_ref, o_ref, *, head_dim):
    TQ = q_ref.shape[0]
    TK = SB_TILE_K
    kpq = TQ // TK
    npair = q_ref.shape[1] // LANES
    nh = 2 * npair
    qi = pl.program_id(2)
    lane = lax.broadcasted_iota(I32, (TQ, LANES), 1)
    first = lane < head_dim
    q_heads = []
    for p in range(npair):
        q = q_ref[:, p * LANES:(p + 1) * LANES].astype(F32)
        q_heads += [jnp.where(first, q, 0.0).astype(BF16), jnp.where(first, 0.0, q).astype(BF16)]
    r = lax.broadcasted_iota(I32, (TK, TK), 0)
    c = lax.broadcasted_iota(I32, (TK, TK), 1)
    later = jnp.where(r >= c, 1.0, 0.0).astype(BF16)
    row = lax.broadcasted_iota(I32, (TQ, TK), 0)
    col = lax.broadcasted_iota(I32, (TQ, TK), 1)

    def block(kb, carry, dk):
        off = pl.multiple_of(kb * TK, TK)
        strict = None if dk is None else (col + dk * TK) < row
        kblk = [k_ref[pl.ds(off, TK), p * LANES:(p + 1) * LANES] for p in range(npair)]
        vblk = [v_ref[pl.ds(off, TK), p * LANES:(p + 1) * LANES] for p in range(npair)]
        zs = [_dot_tb(q_heads[h], kblk[h // 2]) for h in range(nh)]
        ls = []
        for z in zs:
            l = jnp.maximum(z, 0.0) + jnp.log2(1.0 + jnp.exp2(-jnp.abs(z)))
            ls.append(l if strict is None else jnp.where(strict, l, 0.0))
        tots = [_split_dot(ls[h], later) + carry[2 * h + 1] for h in range(nh)]
        new = []
        for h in range(nh):
            a = jnp.exp2(zs[h] - tots[h])
            if strict is not None:
                a = jnp.where(strict, a, 0.0)
            new += [carry[2 * h] + _dot(a.astype(BF16), vblk[h // 2]),
                    jnp.broadcast_to(tots[h][:, 0:1], (TQ, TK))]
        return tuple(new)

    carry = []
    for _ in range(nh):
        carry += [jnp.zeros((TQ, LANES), F32), jnp.zeros((TQ, TK), F32)]
    carry = tuple(carry)
    for dk in reversed(range(kpq)):
        carry = block(qi * kpq + dk, carry, dk)
    carry = lax.fori_loop(0, qi * kpq, lambda j, cr: block(qi * kpq - 1 - j, cr, None), carry)
    for p in range(npair):
        o_ref[:, p * LANES:(p + 1) * LANES] = jnp.where(first, carry[4 * p], carry[4 * p + 2]).astype(o_ref.dtype)


def _sb_attention(q, k, v, *, bsz, seq, head_dim):
    n, d = q.shape
    assert 2 * head_dim == LANES
    t = min(SB_TILE_Q, seq)
    assert t % SB_TILE_K == 0
    w = min(SB_LANES, d)
    q3, k3, v3 = (a.reshape(bsz, seq, d) for a in (q, k, v))
    out = pl.pallas_call(
        functools.partial(_sb_body, head_dim=head_dim),
        out_shape=jax.ShapeDtypeStruct((bsz, seq, d), BF16),
        grid=(bsz, d // w, seq // t),
        in_specs=[
            pl.BlockSpec((None, t, w), lambda b, hp, i: (b, i, hp)),
            pl.BlockSpec((None, seq, w), lambda b, hp, i: (b, 0, hp)),
            pl.BlockSpec((None, seq, w), lambda b, hp, i: (b, 0, hp)),
        ],
        out_specs=pl.BlockSpec((None, t, w), lambda b, hp, i: (b, i, hp)),
        compiler_params=_cparams(("arbitrary", "arbitrary", "arbitrary")),
        name="sb_attention",
    )(q3, k3, v3)
    return out.reshape(n, d)


def _topk_rows(s, k, rows=None, big=None):
    if rows is None:
        rows = lax.broadcasted_iota(I32, s.shape, 0).astype(F32)
        big = float(s.shape[0])
    vals, idxs = [], []
    for _ in range(k):
        m = jnp.max(s, axis=0, keepdims=True)
        i = jnp.min(jnp.where(s == m, rows, big), axis=0, keepdims=True)
        vals.append(m)
        idxs.append(i)
        s = jnp.where(rows == i, -jnp.inf, s)
    return jnp.concatenate(vals, axis=0), jnp.concatenate(idxs, axis=0)


def _select_rows(table, sel, k):
    out = jnp.zeros_like(sel)
    for a in range(k):
        out = jnp.where(sel == float(a), table[a:a + 1, :], out)
    return out


def _route_body(x_ref, g_ref, sh_ref, sc_ref, wq_ref, keys_ref, h_ref, idx_ref, gate_ref,
                q_scr, idx_t, gate_t, *, heads, nkeys):
    k = PEER_TOPK
    tm = x_ref.shape[0]
    hmod = _rmsnorm_mod(x_ref[...], g_ref[...], sh_ref[...], sc_ref[...])
    h_ref[...] = hmod
    q_scr[...] = _dot(hmod.astype(BF16), wq_ref[...]).astype(BF16)
    nsub = tm // ROUTE_SUB
    ncand = sum(k // (a + 1) for a in range(k))
    cand_pad = (-ncand) % SUBLANES
    ridx = lax.broadcasted_iota(I32, (ncand + cand_pad, ROUTE_SUB), 0)
    cand_pos = jnp.full(ridx.shape, k * k, I32)
    start = 0
    for a in range(k):
        nb = k // (a + 1)
        cand_pos = jnp.where((ridx >= start) & (ridx < start + nb), ridx + (a * k - start), cand_pos)
        start += nb
    cand_pos = cand_pos.astype(F32)

    def head_step(it, carry):
        hd = it // nsub
        sub = it % nsub
        tok0 = pl.multiple_of(sub * ROUTE_SUB, ROUTE_SUB)
        tops = []
        for p in range(2):
            c0 = pl.multiple_of((hd * 2 + p) * LANES, LANES)
            qs = q_scr[pl.ds(tok0, ROUTE_SUB), pl.ds(c0, LANES)]
            st = _dot_tb(keys_ref[p], qs)
            tops.append(_topk_rows(st, k))
        (v0, i0), (v1, i1) = tops
        pieces = [v0[a:a + 1, :] + v1[0:k // (a + 1), :] for a in range(k)]
        pieces.append(jnp.full((cand_pad, ROUTE_SUB), -jnp.inf, F32))
        cand = jnp.concatenate(pieces, axis=0)
        g_top, pos = _topk_rows(cand, k, rows=cand_pos, big=float(k * k))
        pa = jnp.floor(pos * (1.0 / k))
        pb = pos - pa * k
        expert = (_select_rows(i0, pa, k) * float(nkeys) + _select_rows(i1, pb, k)) * float(SUBLANES // 2)
        e = jnp.exp(g_top - g_top[0:1, :])
        gate = e / jnp.sum(e, axis=0, keepdims=True)
        r0 = pl.multiple_of(hd * k, k)
        idx_t[pl.ds(r0, k), pl.ds(tok0, ROUTE_SUB)] = expert
        gate_t[pl.ds(r0, k), pl.ds(tok0, ROUTE_SUB)] = gate
        return carry

    lax.fori_loop(0, heads * nsub, head_step, 0)
    idx_ref[...] = idx_t[...].T.astype(I32)
    gate_ref[...] = gate_t[...].T


def _peer_route(x, gain, mod, i_sh, i_sc, wq, keys, *, seq):
    n, d = x.shape
    qcols = wq.shape[1]
    nkeys, kdim = keys.shape[1], keys.shape[2]
    assert kdim == LANES and nkeys == LANES
    heads = qcols // (2 * kdim)
    tm = min(ROUTE_TILE, seq)
    tpb = seq // tm
    ha = heads * PEER_TOPK
    return pl.pallas_call(
        functools.partial(_route_body, heads=heads, nkeys=nkeys),
        out_shape=(jax.ShapeDtypeStruct((n, d), F32),
                   jax.ShapeDtypeStruct((n, ha), I32),
                   jax.ShapeDtypeStruct((n, ha), F32)),
        grid=(n // tm,),
        in_specs=[
            pl.BlockSpec((tm, d), lambda i: (i, 0)),
            pl.BlockSpec((1, d), lambda i: (0, 0)),
            pl.BlockSpec((None, None, 1, d), lambda i: (i // tpb, i_sh, 0, 0)),
            pl.BlockSpec((None, None, 1, d), lambda i: (i // tpb, i_sc, 0, 0)),
            pl.BlockSpec((d, qcols), lambda i: (0, 0)),
            pl.BlockSpec((2, nkeys, kdim), lambda i: (0, 0, 0)),
        ],
        out_specs=(pl.BlockSpec((tm, d), lambda i: (i, 0)),
                   pl.BlockSpec((tm, ha), lambda i: (i, 0)),
                   pl.BlockSpec((tm, ha), lambda i: (i, 0))),
        scratch_shapes=[pltpu.VMEM((tm, qcols), BF16),
                        pltpu.VMEM((ha, tm), F32),
                        pltpu.VMEM((ha, tm), F32)],
        compiler_params=_cparams(("arbitrary",)),
        name="peer_route",
    )(x, gain, mod, mod, wq, keys)


def _pack_table(t):
    e, d = t.shape
    assert d == SUBLANES * LANES
    half = SUBLANES // 2
    bits = lax.bitcast_convert_type(t.astype(BF16).reshape(e, 2, half, LANES), jnp.uint16).astype(jnp.uint32)
    words = lax.bitcast_convert_type(bits[:, 0] | (bits[:, 1] << 16), I32).reshape(e * half, LANES)
    return jnp.pad(words, ((half, half), (0, 0)))


def _window_pair(tbl_ref, row_a, row_b, low_half):
    half = SUBLANES // 2
    wa = tbl_ref[pl.ds(row_a + half, SUBLANES), :]
    wb = tbl_ref[pl.ds(row_b, SUBLANES), :]
    return jnp.where(low_half, wa, wb)


def _unpack(w):
    lo = lax.bitcast_convert_type(w << 16, F32)
    hi = lax.bitcast_convert_type(w & jnp.int32(-65536), F32)
    return lo, hi


def _merge(v1, v2, g, first_half):
    h = g // 2
    a = v1 + pltpu.roll(v1, SUBLANES - h, axis=0)
    b = v2 + pltpu.roll(v2, h, axis=0)
    return jnp.where(first_half, a, b)


def _erf(x):
    ax = jnp.abs(x)
    t = 1.0 / (1.0 + 0.3275911 * ax)
    poly = ((((1.061405429 * t - 1.453152027) * t + 1.421413741) * t - 0.284496736) * t + 0.254829592) * t
    y = 1.0 - poly * jnp.exp(-ax * ax)
    return jnp.where(x < 0, -y, y)


def _peer_u_body(idx_ref, tbl_ref, x_ref, g_ref, w_ref, part_a, part_b):
    tt, na = g_ref.shape
    half = SUBLANES // 2
    sub = lax.broadcasted_iota(I32, (SUBLANES, LANES), 0)
    low_half = sub < half
    first4 = (sub & 3) < 2
    first2 = (sub & 1) < 1
    ones8 = jnp.ones((SUBLANES, LANES), BF16)

    def finish(t, part_ref):
        act = jnp.sum(part_ref[...].T, axis=0, keepdims=True)
        gelu = 0.5 * act * (1.0 + _erf(act * (2.0 ** -0.5)))
        return g_ref[pl.ds(t, 1), :] * gelu

    def gather(t, part_ref):
        xw = x_ref[t]
        xl = jnp.concatenate([xw[0:half], xw[0:half]], axis=0)
        xh = jnp.concatenate([xw[half:], xw[half:]], axis=0)
        groups = []
        for grp in range(na // SUBLANES):
            c = []
            for i in range(half):
                a0 = grp * SUBLANES + BITREV8[2 * i]
                a1 = grp * SUBLANES + BITREV8[2 * i + 1]
                lo, hi = _unpack(_window_pair(tbl_ref, idx_ref[t, a0], idx_ref[t, a1], low_half))
                c.append(lo * xl + hi * xh)
            d0 = _merge(c[0], c[1], 4, first4)
            d1 = _merge(c[2], c[3], 4, first4)
            groups.append(_merge(d0, d1, 2, first2))
        part_ref[...] = jnp.concatenate(groups, axis=0)

    def one_token(t, carry):
        prev = jnp.maximum(t - 1, 0)
        w_prev = finish(prev, part_a)
        gather(t, part_a)
        w_ref[pl.ds(prev, 1), :] = w_prev
        return carry

    part_a[...] = jnp.zeros_like(part_a)
    lax.fori_loop(0, tt, one_token, 0)
    w_ref[pl.ds(tt - 1, 1), :] = finish(tt - 1, part_a)


def _peer_v_body(idx_ref, tbl_ref, w_ref, res_ref, gate_ref, o_ref, rows_a, rows_b, lhs_ref):
    tt = res_ref.shape[0]
    na = idx_ref.shape[1]
    half = SUBLANES // 2
    nk = na * SUBLANES
    lrows = 2 * SUBLANES
    low_half = lax.broadcasted_iota(I32, (SUBLANES, LANES), 0) < half

    w_all = w_ref[...]
    w_hi = w_all.astype(BF16)
    w_lo = (w_all - w_hi.astype(F32)).astype(BF16)
    a_of = lax.broadcasted_iota(I32, (na, nk), 0)
    kk = lax.broadcasted_iota(I32, (na, nk), 1)
    rep = jnp.where(a_of == 2 * (kk // 16) + (kk % 16) // 8, 1.0, 0.0).astype(BF16)
    chunk = LANES // lrows
    r = lax.broadcasted_iota(I32, (LANES, tt), 0)
    c = lax.broadcasted_iota(I32, (LANES, tt), 1)
    srow = lax.broadcasted_iota(I32, (LANES, nk), 0) % SUBLANES
    kk2 = lax.broadcasted_iota(I32, (LANES, nk), 1)
    keep = srow == ((kk2 % 16) // 2) % 4 + 4 * (kk2 % 2)
    w_rows = []
    for q in range(tt // chunk):
        tok_of = q * chunk + r // lrows
        sel_hi = jnp.where((tok_of == c) & (r % lrows < SUBLANES), 1.0, 0.0).astype(BF16)
        sel_lo = jnp.where((tok_of == c) & (r % lrows >= SUBLANES), 1.0, 0.0).astype(BF16)
        w_rows.append((_dot(sel_hi, w_hi) + _dot(sel_lo, w_lo)).astype(BF16))
    for q in range(tt // chunk):
        lhs_ref[q * LANES:(q + 1) * LANES, :] = jnp.where(keep, _dot(w_rows[q], rep), 0.0).astype(BF16)

    def gather(t, rows_ref):
        for j in range(na // 2):
            tile = _window_pair(tbl_ref, idx_ref[t, 2 * j], idx_ref[t, 2 * j + 1], low_half)
            rows_ref[j * SUBLANES:(j + 1) * SUBLANES, :] = tile

    def finish(t, rows_ref):
        lhs = lhs_ref[pl.ds(pl.multiple_of(t * lrows, lrows), lrows), :]
        out = _dot(lhs, pltpu.bitcast(rows_ref[...], BF16))
        return res_ref[t] + gate_ref[...] * (out[0:SUBLANES] + out[SUBLANES:])

    def two_tokens(i, carry):
        t = 2 * i
        p0 = jnp.maximum(t - 2, 0)
        p1 = jnp.maximum(t - 1, 0)
        o0 = finish(p0, rows_a)
        o1 = finish(p1, rows_b)
        gather(t, rows_a)
        gather(t + 1, rows_b)
        o_ref[p0] = o0
        o_ref[p1] = o1
        return carry

    rows_a[...] = jnp.zeros_like(rows_a)
    rows_b[...] = jnp.zeros_like(rows_b)
    lax.fori_loop(0, tt // 2, two_tokens, 0)
    o_ref[tt - 2] = finish(tt - 2, rows_a)
    o_ref[tt - 1] = finish(tt - 1, rows_b)


def _peer_experts(x, hmod, idx, gate, tbl_u, tbl_v, mod8, i_gate, *, seq):
    n, d = x.shape
    na = idx.shape[1]
    tt = min(PEER_TILE, seq)
    tpb = seq // tt
    x8 = x.reshape(n, SUBLANES, LANES)
    h8 = hmod.reshape(n, SUBLANES, LANES)
    smem = functools.partial(pl.BlockSpec, memory_space=pltpu.SMEM)
    tbl_spec = pl.BlockSpec(memory_space=pltpu.VMEM)
    w = pl.pallas_call(
        _peer_u_body,
        out_shape=jax.ShapeDtypeStruct((n, na), F32),
        grid=(n // tt,),
        in_specs=[
            smem((tt, na), lambda i: (i, 0)),
            tbl_spec,
            pl.BlockSpec((tt, SUBLANES, LANES), lambda i: (i, 0, 0)),
            pl.BlockSpec((tt, na), lambda i: (i, 0)),
        ],
        out_specs=pl.BlockSpec((tt, na), lambda i: (i, 0)),
        scratch_shapes=[pltpu.VMEM((na, LANES), F32), pltpu.VMEM((na, LANES), F32)],
        compiler_params=_cparams(("arbitrary",)),
        name="peer_u",
    )(idx, tbl_u, h8, gate)
    out = pl.pallas_call(
        _peer_v_body,
        out_shape=jax.ShapeDtypeStruct((n, SUBLANES, LANES), F32),
        grid=(n // tt,),
        in_specs=[
            smem((tt, na), lambda i: (i, 0)),
            tbl_spec,
            pl.BlockSpec((tt, na), lambda i: (i, 0)),
            pl.BlockSpec((tt, SUBLANES, LANES), lambda i: (i, 0, 0)),
            pl.BlockSpec((None, None, SUBLANES, LANES), lambda i: (i // tpb, i_gate, 0, 0)),
        ],
        out_specs=pl.BlockSpec((tt, SUBLANES, LANES), lambda i: (i, 0, 0)),
        scratch_shapes=[pltpu.VMEM((na * SUBLANES // 2, LANES), I32), pltpu.VMEM((na * SUBLANES // 2, LANES), I32),
                        pltpu.VMEM((tt * 2 * SUBLANES, na * SUBLANES), BF16)],
        compiler_params=_cparams(("arbitrary",)),
        name="peer_v",
    )(idx, tbl_v, w, x8, mod8)
    return out.reshape(n, d)


def _peer_ffn(x, gain, mod, wq, sub_keys, u, v, *, seq):
    bsz = mod.shape[0]
    d = x.shape[1]
    hmod, idx, gate = _peer_route(x, gain, mod, 3, 4, wq.astype(BF16), sub_keys.astype(BF16), seq=seq)
    mod8 = mod.reshape(bsz, mod.shape[1], SUBLANES, LANES)
    return _peer_experts(x, hmod, idx, gate, _pack_table(u), _pack_table(v), mod8, 5, seq=seq)


def kernel(x, c, ada_w, ada_b, norm_mix_g, norm_ffn_g, ma_w_in, ma_conv_w, ma_b_if, ma_hnorm_g, ma_w_out,
           kv_ada_w, kv_ada_b, kv_norm_g, kv_w, k_norm_g, sb_w_q, sb_q_norm_g, sb_w_out,
           peer_w_q, peer_sub_keys, peer_u, peer_v):
    bsz, seq, d = x.shape
    n = bsz * seq
    depth = ada_w.shape[0]
    n_a = ma_w_in.shape[0]
    sb_dim = k_norm_g.shape[0]
    sb_heads = d // sb_dim

    mods = _ada(c, ada_w, ada_b).reshape(depth, bsz, 6, 1, d)
    kv_mod = _ada(c, kv_ada_w[None], kv_ada_b[None]).reshape(bsz, 2, 1, d)
    xs = x.reshape(n, d)
    k_sh = v_sh = None
    for layer in range(depth):
        mod = mods[layer]
        g_mix = norm_mix_g[layer].reshape(1, d)
        if layer < n_a:
            w_in = ma_w_in[layer]
            pad = (-w_in.shape[1]) % LANES
            w_in = jnp.pad(w_in, ((0, 0), (0, pad))).astype(BF16)
            ncols = w_in.shape[1]
            tn = ncols // 5 if ncols % (5 * LANES) == 0 else None
            proj = _nmm(xs, w_in, seq=seq, norm=(g_mix, mod, 0, 1), tn=tn, name="mlstm_proj")
            hh = _mlstm(proj, ma_conv_w[layer], ma_b_if[layer], ma_hnorm_g[layer], bsz=bsz, seq=seq, d=d)
            xs = _nmm(hh, ma_w_out[layer].astype(BF16), seq=seq, resid=(xs, mod, 2), name="mlstm_out")
        else:
            if layer == n_a:
                kvw = kv_w.astype(BF16)
                k_gain = jnp.tile(k_norm_g, sb_heads).reshape(1, d)
                g_kv = kv_norm_g.reshape(1, d)
                k_sh = _nmm(xs, kvw[:, :d], seq=seq, norm=(g_kv, kv_mod, 0, 1),
                            headnorm=(k_gain, sb_dim), out_dtype=BF16, name="kv_k")
                v_sh = _nmm(xs, kvw[:, d:], seq=seq, norm=(g_kv, kv_mod, 0, 1), out_dtype=BF16, name="kv_v")
            j = layer - n_a
            q_gain = (jnp.tile(sb_q_norm_g[j], sb_heads) * (sb_dim ** -0.5 * LOG2E)).reshape(1, d)
            q = _nmm(xs, sb_w_q[j].astype(BF16), seq=seq, norm=(g_mix, mod, 0, 1),
                     headnorm=(q_gain, sb_dim), out_dtype=BF16, name="sb_q")
            o = _sb_attention(q, k_sh, v_sh, bsz=bsz, seq=seq, head_dim=sb_dim)
            xs = _nmm(o, sb_w_out[j].astype(BF16), seq=seq, resid=(xs, mod, 2), name="sb_out")
        xs = _peer_ffn(xs, norm_ffn_g[layer].reshape(1, d), mod, peer_w_q[layer], peer_sub_keys[layer],
                       peer_u[layer], peer_v[layer], seq=seq)
    return xs.reshape(bsz, seq, d)
```

```python
import functools

import jax
import jax.numpy as jnp
import numpy as np
from jax import lax
from jax.experimental import pallas as pl
from jax.experimental.pallas import tpu as pltpu

F32 = jnp.float32
BF16 = jnp.bfloat16
I32 = jnp.int32

EPS = 1e-6
PEER_TOPK = 16
LANES = 128
SUBLANES = 8
VMEM_LIMIT_BYTES = 56 * 1024 * 1024

MLSTM_CHUNK = 256
SB_TILE_Q = 256
SB_TILE_K = 128
SB_LANES = 512
LOG2E = 1.4426950408889634
MM_TILE_M = 512
ROUTE_TILE = 256
ROUTE_SUB = 128
PEER_TILE = 64
PEER_GROUP = 4
IDX_SPLIT = 8
BITREV8 = (0, 4, 2, 6, 1, 5, 3, 7)


def _cparams(sem):
    return pltpu.CompilerParams(dimension_semantics=sem, vmem_limit_bytes=VMEM_LIMIT_BYTES)


def _dot(a, b):
    return jnp.dot(a, b, preferred_element_type=F32)


def _dot_tb(a, b):
    return lax.dot_general(a, b, (((1,), (1,)), ((), ())), preferred_element_type=F32)


def _split_dot(x, m):
    hi = x.astype(BF16)
    lo = (x - hi.astype(F32)).astype(BF16)
    return _dot(hi, m) + _dot(lo, m)


def _sigmoid(x):
    return 1.0 / (1.0 + jnp.exp(-x))


def _softplus(x):
    return jnp.maximum(x, 0.0) + jnp.log1p(jnp.exp(-jnp.abs(x)))


def _rmsnorm_mod(x, g, shift, scale):
    ms = jnp.mean(x * x, axis=-1, keepdims=True)
    y = x * lax.rsqrt(ms + EPS) * g
    return y * (1.0 + scale) + shift


def _ada_body(c_ref, w_ref, b_ref, o_ref):
    c = c_ref[...]
    a = (c * _sigmoid(c)).astype(BF16)
    o_ref[...] = _dot(a, w_ref[...].astype(BF16)) + b_ref[...]


def _ada(c, w, b):
    nl, d, e = w.shape
    bsz = c.shape[0]
    te = 1024
    return pl.pallas_call(
        _ada_body,
        out_shape=jax.ShapeDtypeStruct((nl, bsz, e), F32),
        grid=(nl, e // te),
        in_specs=[
            pl.BlockSpec((bsz, d), lambda l, j: (0, 0)),
            pl.BlockSpec((None, d, te), lambda l, j: (l, 0, j)),
            pl.BlockSpec((None, 1, te), lambda l, j: (l, 0, j)),
        ],
        out_specs=pl.BlockSpec((None, bsz, te), lambda l, j: (l, 0, j)),
        compiler_params=_cparams(("arbitrary", "arbitrary")),
        name="ada_mod",
    )(c, w, b.reshape(nl, 1, e))


def _nmm_body(*refs, prologue, epilogue, head_dim):
    it = iter(refs)
    x_ref = next(it)
    if prologue:
        g_ref, sh_ref, sc_ref = next(it), next(it), next(it)
    w_ref = next(it)
    if epilogue == "resid":
        res_ref, gate_ref = next(it), next(it)
    if epilogue == "headnorm":
        hg_ref = next(it)
    o_ref = next(it)
    h_ref = next(it)

    @pl.when(pl.program_id(1) == 0)
    def _():
        x = x_ref[...]
        if prologue:
            x = _rmsnorm_mod(x, g_ref[...], sh_ref[...], sc_ref[...])
        h_ref[...] = x.astype(BF16)

    acc = _dot(h_ref[...], w_ref[...])
    if epilogue == "resid":
        acc = res_ref[...] + gate_ref[...] * acc
    elif epilogue == "headnorm":
        r = lax.broadcasted_iota(I32, (LANES, LANES), 0) // head_dim
        c = lax.broadcasted_iota(I32, (LANES, LANES), 1) // head_dim
        group = jnp.where(r == c, 1.0, 0.0).astype(BF16)
        sq = acc * acc
        ms = jnp.concatenate(
            [_split_dot(sq[:, j * LANES:(j + 1) * LANES], group) for j in range(acc.shape[1] // LANES)],
            axis=1) * (1.0 / head_dim)
        acc = acc * lax.rsqrt(ms + EPS) * hg_ref[...]
    o_ref[...] = acc.astype(o_ref.dtype)


def _nmm(x, w, *, seq, norm=None, resid=None, headnorm=None, out_dtype=F32, tn=None, name):
    n, d = x.shape
    e = w.shape[1]
    tm = min(MM_TILE_M, seq)
    tn = e if tn is None else tn
    tpb = seq // tm
    args = [x]
    in_specs = [pl.BlockSpec((tm, d), lambda i, j: (i, 0))]
    if norm is not None:
        gain, mod, i_sh, i_sc = norm
        args += [gain, mod, mod]
        in_specs += [
            pl.BlockSpec((1, d), lambda i, j: (0, 0)),
            pl.BlockSpec((None, None, 1, d), lambda i, j: (i // tpb, i_sh, 0, 0)),
            pl.BlockSpec((None, None, 1, d), lambda i, j: (i // tpb, i_sc, 0, 0)),
        ]
    args.append(w)
    in_specs.append(pl.BlockSpec((d, tn), lambda i, j: (0, j)))
    epilogue = None
    head_dim = 0
    if resid is not None:
        res, mod, i_g = resid
        epilogue = "resid"
        args += [res, mod]
        in_specs += [
            pl.BlockSpec((tm, tn), lambda i, j: (i, j)),
            pl.BlockSpec((None, None, 1, tn), lambda i, j: (i // tpb, i_g, 0, j)),
        ]
    if headnorm is not None:
        hg, head_dim = headnorm
        epilogue = "headnorm"
        args.append(hg)
        in_specs.append(pl.BlockSpec((1, tn), lambda i, j: (0, j)))
    return pl.pallas_call(
        functools.partial(_nmm_body, prologue=norm is not None, epilogue=epilogue, head_dim=head_dim),
        out_shape=jax.ShapeDtypeStruct((n, e), out_dtype),
        grid=(n // tm, e // tn),
        in_specs=in_specs,
        out_specs=pl.BlockSpec((tm, tn), lambda i, j: (i, j)),
        scratch_shapes=[pltpu.VMEM((tm, d), BF16)],
        compiler_params=_cparams(("arbitrary", "arbitrary")),
        name=name,
    )(*args)


def _mlstm_body(qk_ref, v_ref, o_ref, gt_ref, cw_ref, bif_ref, hg_ref, out_ref,
                tail_ref, c_ref, m_ref, *, heads, qk_dim, conv_w):
    L = qk_ref.shape[0]
    nqk = heads * qk_dim
    vdim = LANES

    @pl.when(pl.program_id(1) == 0)
    def _():
        tail_ref[...] = jnp.zeros_like(tail_ref)
        c_ref[...] = jnp.zeros_like(c_ref)
        m_ref[...] = jnp.zeros_like(m_ref)

    x = qk_ref[...]
    tail = tail_ref[...]
    rows8 = lax.broadcasted_iota(I32, (SUBLANES, x.shape[1]), 0)
    acc = x * cw_ref[conv_w - 1:conv_w, :]
    for j in range(1, conv_w):
        xs = pltpu.roll(x, j, axis=0)
        head8 = jnp.where(rows8 < j, pltpu.roll(tail, j, axis=0), xs[0:SUBLANES])
        xs = jnp.concatenate([head8, xs[SUBLANES:]], axis=0)
        acc = acc + xs * cw_ref[conv_w - 1 - j:conv_w - j, :]
    tail_ref[...] = x[L - SUBLANES:, :]
    qk = acc * _sigmoid(acc)
    q_all = qk[:, :nqk] * (qk_dim ** -0.5)
    k_all = qk[:, nqk:]
    kt_all = k_all.T.astype(BF16)

    gates = gt_ref[...] + bif_ref[...]
    log_f = -_softplus(-gates)
    r = lax.broadcasted_iota(I32, (L, L), 0)
    c = lax.broadcasted_iota(I32, (L, L), 1)
    causal = r >= c
    tri = jnp.where(causal, 1.0, 0.0).astype(BF16)
    lf_hi = log_f.astype(BF16)
    lf_lo = (log_f - lf_hi.astype(F32)).astype(BF16)
    bcum = _dot(tri, lf_hi) + _dot(tri, lf_lo)
    gates_t = gates.T
    bcum_t = bcum.T
    lane = lax.broadcasted_iota(I32, (L, LANES), 1)
    ones_v = jnp.ones((L, vdim), BF16)
    rows_c = lax.broadcasted_iota(I32, (LANES, 2 * vdim), 0)

    for h in range(heads):
        pair, half = h // 2, h % 2
        b_col = bcum[:, heads + h:heads + h + 1]
        i_col = gates[:, h:h + 1]
        b_row = bcum_t[heads + h:heads + h + 1, :]
        i_row = gates_t[h:h + 1, :]
        b_last = b_col[L - 1:L, :]
        m_prev = m_ref[h:h + 1, 0:1]
        in_head = (lane >= half * qk_dim) & (lane < (half + 1) * qk_dim)
        q_h = jnp.where(in_head, q_all[:, pair * LANES:(pair + 1) * LANES], 0.0).astype(BF16)
        kt_h = kt_all[pair * LANES:(pair + 1) * LANES, :]
        v_h = v_ref[:, h * vdim:(h + 1) * vdim]
        v_aug = jnp.concatenate([v_h.astype(BF16), ones_v], axis=1)

        log_d = jnp.where(causal, b_col - b_row + i_row, -jnp.inf)
        inter_log = b_col + m_prev
        m_t = jnp.maximum(inter_log, jnp.max(log_d, axis=1, keepdims=True))
        inter_w = jnp.exp(inter_log - m_t)
        s = _dot(q_h, kt_h) * jnp.exp(log_d - m_t)
        c_prev = c_ref[h]
        tot = _dot(s.astype(BF16), v_aug) + inter_w * _dot(q_h, c_prev.astype(BF16))
        num = tot[:, :vdim]
        den = tot[:, vdim:]
        hh = num / jnp.maximum(jnp.abs(den), jnp.exp(-m_t))
        ms = jnp.mean(hh * hh, axis=1, keepdims=True)
        hh = hh * lax.rsqrt(ms + EPS) * hg_ref[:, h * vdim:(h + 1) * vdim]
        hh = hh * _sigmoid(o_ref[:, h * vdim:(h + 1) * vdim])
        out_ref[:, h * vdim:(h + 1) * vdim] = hh.astype(out_ref.dtype)

        w_state = b_last - b_col + i_col
        m_loc = jnp.max(w_state, axis=0, keepdims=True)
        e_state = jnp.exp(w_state - m_loc)
        ev = (e_state * jnp.concatenate([v_h, jnp.ones((L, vdim), F32)], axis=1)).astype(BF16)
        c_loc = _dot(kt_h, ev)
        in_rows = (rows_c >= half * qk_dim) & (rows_c < (half + 1) * qk_dim)
        c_loc = jnp.where(in_rows, c_loc, 0.0)
        m_new = jnp.maximum(b_last + m_prev, m_loc)
        a = jnp.exp(b_last + m_prev - m_new)
        rr = jnp.exp(m_loc - m_new)
        c_ref[h] = a * c_prev + rr * c_loc
        m_ref[h:h + 1, :] = jnp.broadcast_to(m_new, (1, LANES))


def _mlstm(proj, conv_w, b_if, hnorm_g, *, bsz, seq, d):
    heads, vdim = hnorm_g.shape
    assert vdim == LANES
    nqk2 = conv_w.shape[1]
    qk_dim = nqk2 // (2 * heads)
    width = conv_w.shape[0]
    L = min(MLSTM_CHUNK, seq)
    nc = seq // L
    bif = jnp.zeros((1, LANES), F32).at[0, :2 * heads].set(b_if)
    gate_blk = (nqk2 + 2 * d) // LANES
    return pl.pallas_call(
        functools.partial(_mlstm_body, heads=heads, qk_dim=qk_dim, conv_w=width),
        out_shape=jax.ShapeDtypeStruct((bsz * seq, d), BF16),
        grid=(bsz, nc),
        in_specs=[
            pl.BlockSpec((L, nqk2), lambda b, c: (b * nc + c, 0)),
            pl.BlockSpec((L, d), lambda b, c: (b * nc + c, nqk2 // d)),
            pl.BlockSpec((L, d), lambda b, c: (b * nc + c, nqk2 // d + 1)),
            pl.BlockSpec((L, LANES), lambda b, c: (b * nc + c, gate_blk)),
            pl.BlockSpec((width, nqk2), lambda b, c: (0, 0)),
            pl.BlockSpec((1, LANES), lambda b, c: (0, 0)),
            pl.BlockSpec((1, d), lambda b, c: (0, 0)),
        ],
        out_specs=pl.BlockSpec((L, d), lambda b, c: (b * nc + c, 0)),
        scratch_shapes=[
            pltpu.VMEM((SUBLANES, nqk2), F32),
            pltpu.VMEM((heads, LANES, 2 * LANES), F32),
            pltpu.VMEM((heads, LANES), F32),
        ],
        compiler_params=_cparams(("arbitrary", "arbitrary")),
        name="mlstm",
    )(proj, proj, proj, proj, conv_w, bif, hnorm_g.reshape(1, d))


def _sb_body(q_ref, k_ref, v_ref, o_ref, *, head_dim):
    TQ = q_ref.shape[0]
    TK = SB_TILE_K
    kpq = TQ // TK
    npair = q_ref.shape[1] // LANES
    nh = 2 * npair
    qi = pl.program_id(2)
    lane = lax.broadcasted_iota(I32, (TQ, LANES), 1)
    first = lane < head_dim
    q_heads = []
    for p in range(npair):
        q = q_ref[:, p * LANES:(p + 1) * LANES].astype(F32)
        q_heads += [jnp.where(first, q, 0.0).astype(BF16), jnp.where(first, 0.0, q).astype(BF16)]
    r = lax.broadcasted_iota(I32, (TK, TK), 0)
    c = lax.broadcasted_iota(I32, (TK, TK), 1)
    later = jnp.where(r >= c, 1.0, 0.0).astype(BF16)
    row = lax.broadcasted_iota(I32, (TQ, TK), 0)
    col = lax.broadcasted_iota(I32, (TQ, TK), 1)

    def block(kb, carry, dk):
        off = pl.multiple_of(kb * TK, TK)
        strict = None if dk is None else (col + dk * TK) < row
        kblk = [k_ref[pl.ds(off, TK), p * LANES:(p + 1) * LANES] for p in range(npair)]
        vblk = [v_ref[pl.ds(off, TK), p * LANES:(p + 1) * LANES] for p in range(npair)]
        zs = [_dot_tb(q_heads[h], kblk[h // 2]) for h in range(nh)]
        ls = []
        for z in zs:
            l = jnp.maximum(z, 0.0) + jnp.log2(1.0 + jnp.exp2(-jnp.abs(z)))
            ls.append(l if strict is None else jnp.where(strict, l, 0.0))
        tots = [_split_dot(ls[h], later) + carry[2 * h + 1] for h in range(nh)]
        new = []
        for h in range(nh):
            a = jnp.exp2(zs[h] - tots[h])
            if strict is not None:
                a = jnp.where(strict, a, 0.0)
            new += [carry[2 * h] + _dot(a.astype(BF16), vblk[h // 2]),
                    jnp.broadcast_to(tots[h][:, 0:1], (TQ, TK))]
        return tuple(new)

    carry = []
    for _ in range(nh):
        carry += [jnp.zeros((TQ, LANES), F32), jnp.zeros((TQ, TK), F32)]
    carry = tuple(carry)
    for dk in reversed(range(kpq)):
        carry = block(qi * kpq + dk, carry, dk)
    carry = lax.fori_loop(0, qi * kpq, lambda j, cr: block(qi * kpq - 1 - j, cr, None), carry)
    for p in range(npair):
        o_ref[:, p * LANES:(p + 1) * LANES] = jnp.where(first, carry[4 * p], carry[4 * p + 2]).astype(o_ref.dtype)


def _sb_attention(q, k, v, *, bsz, seq, head_dim):
    n, d = q.shape
    assert 2 * head_dim == LANES
    t = min(SB_TILE_Q, seq)
    assert t % SB_TILE_K == 0
    w = min(SB_LANES, d)
    q3, k3, v3 = (a.reshape(bsz, seq, d) for a in (q, k, v))
    out = pl.pallas_call(
        functools.partial(_sb_body, head_dim=head_dim),
        out_shape=jax.ShapeDtypeStruct((bsz, seq, d), BF16),
        grid=(bsz, d // w, seq // t),
        in_specs=[
            pl.BlockSpec((None, t, w), lambda b, hp, i: (b, i, hp)),
            pl.BlockSpec((None, seq, w), lambda b, hp, i: (b, 0, hp)),
            pl.BlockSpec((None, seq, w), lambda b, hp, i: (b, 0, hp)),
        ],
        out_specs=pl.BlockSpec((None, t, w), lambda b, hp, i: (b, i, hp)),
        compiler_params=_cparams(("arbitrary", "arbitrary", "arbitrary")),
        name="sb_attention",
    )(q3, k3, v3)
    return out.reshape(n, d)


def _topk_rows(s, k, rows=None, big=None):
    if rows is None:
        rows = lax.broadcasted_iota(I32, s.shape, 0).astype(F32)
        big = float(s.shape[0])
    vals, idxs = [], []
    for _ in range(k):
        m = jnp.max(s, axis=0, keepdims=True)
        i = jnp.min(jnp.where(s == m, rows, big), axis=0, keepdims=True)
        vals.append(m)
        idxs.append(i)
        s = jnp.where(rows == i, -jnp.inf, s)
    return jnp.concatenate(vals, axis=0), jnp.concatenate(idxs, axis=0)


def _select_rows(table, sel, k):
    out = jnp.zeros_like(sel)
    for a in range(k):
        out = jnp.where(sel == float(a), table[a:a + 1, :], out)
    return out


def _route_body(x_ref, g_ref, sh_ref, sc_ref, wq_ref, keys_ref, h_ref, idx_ref, gate_ref,
                q_scr, idx_t, gate_t, *, heads, nkeys):
    k = PEER_TOPK
    tm = x_ref.shape[0]
    hmod = _rmsnorm_mod(x_ref[...], g_ref[...], sh_ref[...], sc_ref[...])
    h_ref[...] = hmod
    q_scr[...] = _dot(hmod.astype(BF16), wq_ref[...]).astype(BF16)
    nsub = tm // ROUTE_SUB
    ncand = sum(k // (a + 1) for a in range(k))
    cand_pad = (-ncand) % SUBLANES
    ridx = lax.broadcasted_iota(I32, (ncand + cand_pad, ROUTE_SUB), 0)
    cand_pos = jnp.full(ridx.shape, k * k, I32)
    start = 0
    for a in range(k):
        nb = k // (a + 1)
        cand_pos = jnp.where((ridx >= start) & (ridx < start + nb), ridx + (a * k - start), cand_pos)
        start += nb
    cand_pos = cand_pos.astype(F32)

    def head_step(it, carry):
        hd = it // nsub
        sub = it % nsub
        tok0 = pl.multiple_of(sub * ROUTE_SUB, ROUTE_SUB)
        tops = []
        for p in range(2):
            c0 = pl.multiple_of((hd * 2 + p) * LANES, LANES)
            qs = q_scr[pl.ds(tok0, ROUTE_SUB), pl.ds(c0, LANES)]
            st = _dot_tb(keys_ref[p], qs)
            tops.append(_topk_rows(st, k))
        (v0, i0), (v1, i1) = tops
        pieces = [v0[a:a + 1, :] + v1[0:k // (a + 1), :] for a in range(k)]
        pieces.append(jnp.full((cand_pad, ROUTE_SUB), -jnp.inf, F32))
        cand = jnp.concatenate(pieces, axis=0)
        g_top, pos = _topk_rows(cand, k, rows=cand_pos, big=float(k * k))
        pa = jnp.floor(pos * (1.0 / k))
        pb = pos - pa * k
        expert = (_select_rows(i0, pa, k) * float(nkeys) + _select_rows(i1, pb, k)) * float(SUBLANES // 2)
        e = jnp.exp(g_top - g_top[0:1, :])
        gate = e / jnp.sum(e, axis=0, keepdims=True)
        r0 = pl.multiple_of(hd * k, k)
        idx_t[pl.ds(r0, k), pl.ds(tok0, ROUTE_SUB)] = expert
        gate_t[pl.ds(r0, k), pl.ds(tok0, ROUTE_SUB)] = gate
        return carry

    lax.fori_loop(0, heads * nsub, head_step, 0)
    idx_ref[...] = idx_t[...].T.astype(I32)
    gate_ref[...] = gate_t[...].T


def _peer_route(x, gain, mod, i_sh, i_sc, wq, keys, *, seq):
    n, d = x.shape
    qcols = wq.shape[1]
    nkeys, kdim = keys.shape[1], keys.shape[2]
    assert kdim == LANES and nkeys == LANES
    heads = qcols // (2 * kdim)
    tm = min(ROUTE_TILE, seq)
    tpb = seq // tm
    ha = heads * PEER_TOPK
    return pl.pallas_call(
        functools.partial(_route_body, heads=heads, nkeys=nkeys),
        out_shape=(jax.ShapeDtypeStruct((n, d), F32),
                   jax.ShapeDtypeStruct((n, ha), I32),
                   jax.ShapeDtypeStruct((n, ha), F32)),
        grid=(n // tm,),
        in_specs=[
            pl.BlockSpec((tm, d), lambda i: (i, 0)),
            pl.BlockSpec((1, d), lambda i: (0, 0)),
            pl.BlockSpec((None, None, 1, d), lambda i: (i // tpb, i_sh, 0, 0)),
            pl.BlockSpec((None, None, 1, d), lambda i: (i // tpb, i_sc, 0, 0)),
            pl.BlockSpec((d, qcols), lambda i: (0, 0)),
            pl.BlockSpec((2, nkeys, kdim), lambda i: (0, 0, 0)),
        ],
        out_specs=(pl.BlockSpec((tm, d), lambda i: (i, 0)),
                   pl.BlockSpec((tm, ha), lambda i: (i, 0)),
                   pl.BlockSpec((tm, ha), lambda i: (i, 0))),
        scratch_shapes=[pltpu.VMEM((tm, qcols), BF16),
                        pltpu.VMEM((ha, tm), F32),
                        pltpu.VMEM((ha, tm), F32)],
        compiler_params=_cparams(("arbitrary",)),
        name="peer_route",
    )(x, gain, mod, mod, wq, keys)


def _pack_table(t):
    e, d = t.shape
    assert d == SUBLANES * LANES
    half = SUBLANES // 2
    bits = lax.bitcast_convert_type(t.astype(BF16).reshape(e, 2, half, LANES), jnp.uint16).astype(jnp.uint32)
    words = lax.bitcast_convert_type(bits[:, 0] | (bits[:, 1] << 16), I32).reshape(e * half, LANES)
    return jnp.pad(words, ((half, half), (0, 0)))


def _row_of(idx_refs, t, a):
    return idx_refs[a % IDX_SPLIT][t, a // IDX_SPLIT]


def _window_pair(tbl_ref, row_a, row_b, low_half):
    half = SUBLANES // 2
    wa = tbl_ref[pl.ds(row_a + half, SUBLANES), :]
    wb = tbl_ref[pl.ds(row_b, SUBLANES), :]
    return jnp.where(low_half, wa, wb)


def _unpack(w):
    lo = lax.bitcast_convert_type(w << 16, F32)
    hi = lax.bitcast_convert_type(w & jnp.int32(-65536), F32)
    return lo, hi


def _merge(v1, v2, g, first_half):
    h = g // 2
    a = v1 + pltpu.roll(v1, SUBLANES - h, axis=0)
    b = v2 + pltpu.roll(v2, h, axis=0)
    return jnp.where(first_half, a, b)


def _erf(x):
    ax = jnp.abs(x)
    t = 1.0 / (1.0 + 0.3275911 * ax)
    poly = ((((1.061405429 * t - 1.453152027) * t + 1.421413741) * t - 0.284496736) * t + 0.254829592) * t
    y = 1.0 - poly * jnp.exp(-ax * ax)
    return jnp.where(x < 0, -y, y)


def _peer_u_body(*refs):
    idx_refs = refs[:IDX_SPLIT]
    tbl_ref, x_ref, g_ref, w_ref = refs[IDX_SPLIT:IDX_SPLIT + 4]
    parts = refs[IDX_SPLIT + 4:]
    tt, na = g_ref.shape
    half = SUBLANES // 2
    sub = lax.broadcasted_iota(I32, (SUBLANES, LANES), 0)
    low_half = sub < half
    first4 = (sub & 3) < 2
    first2 = (sub & 1) < 1
    ones8 = jnp.ones((SUBLANES, LANES), BF16)

    def finish(t, part_ref):
        act = jnp.sum(part_ref[...].T, axis=0, keepdims=True)
        gelu = 0.5 * act * (1.0 + _erf(act * (2.0 ** -0.5)))
        return g_ref[pl.ds(t, 1), :] * gelu

    def gather(t, part_ref):
        xw = x_ref[t]
        xl = jnp.concatenate([xw[0:half], xw[0:half]], axis=0)
        xh = jnp.concatenate([xw[half:], xw[half:]], axis=0)
        groups = []
        for grp in range(na // SUBLANES):
            c = []
            for i in range(half):
                a0 = grp * SUBLANES + BITREV8[2 * i]
                a1 = grp * SUBLANES + BITREV8[2 * i + 1]
                lo, hi = _unpack(_window_pair(tbl_ref, _row_of(idx_refs, t, a0), _row_of(idx_refs, t, a1),
                                              low_half))
                c.append(lo * xl + hi * xh)
            d0 = _merge(c[0], c[1], 4, first4)
            d1 = _merge(c[2], c[3], 4, first4)
            groups.append(_merge(d0, d1, 2, first2))
        part_ref[...] = jnp.concatenate(groups, axis=0)

    grp = len(parts)

    def token_group(i, carry):
        t = grp * i
        prev = [jnp.maximum(t - grp + k, 0) for k in range(grp)]
        done = [finish(prev[k], parts[k]) for k in range(grp)]
        for k in range(grp):
            gather(t + k, parts[k])
        for k in range(grp):
            w_ref[pl.ds(prev[k], 1), :] = done[k]
        return carry

    for p in parts:
        p[...] = jnp.zeros_like(p)
    lax.fori_loop(0, tt // grp, token_group, 0)
    for k in range(grp):
        w_ref[pl.ds(tt - grp + k, 1), :] = finish(tt - grp + k, parts[k])


def _peer_v_body(*refs):
    idx_refs = refs[:IDX_SPLIT]
    tbl_ref, rep_ref, keep_ref, w_ref, res_ref, gate_ref, o_ref, spread_ref = refs[IDX_SPLIT:IDX_SPLIT + 8]
    rows = refs[IDX_SPLIT + 8:]
    tt = res_ref.shape[0]
    na = w_ref.shape[1]
    half = SUBLANES // 2
    low_half = lax.broadcasted_iota(I32, (SUBLANES, LANES), 0) < half

    w_all = w_ref[...]
    w_hi = w_all.astype(BF16)
    w_lo = (w_all - w_hi.astype(F32)).astype(BF16)
    spread_ref[...] = _dot(jnp.concatenate([w_hi, w_lo], axis=0), rep_ref[...])
    keep = keep_ref[...]

    def gather(t, rows_ref):
        for j in range(na // 2):
            tile = _window_pair(tbl_ref, _row_of(idx_refs, t, 2 * j), _row_of(idx_refs, t, 2 * j + 1), low_half)
            rows_ref[j * SUBLANES:(j + 1) * SUBLANES, :] = tile

    def finish(t, rows_ref):
        lhs = jnp.concatenate([spread_ref[pl.ds(t, 1), :] * keep, spread_ref[pl.ds(tt + t, 1), :] * keep],
                              axis=0).astype(BF16)
        out = _dot(lhs, pltpu.bitcast(rows_ref[...], BF16))
        return res_ref[t] + gate_ref[...] * (out[0:SUBLANES] + out[SUBLANES:])

    grp = len(rows)

    def token_group(i, carry):
        t = grp * i
        prev = [jnp.maximum(t - grp + k, 0) for k in range(grp)]
        done = [finish(prev[k], rows[k]) for k in range(grp)]
        for k in range(grp):
            gather(t + k, rows[k])
        for k in range(grp):
            o_ref[prev[k]] = done[k]
        return carry

    for rw in rows:
        rw[...] = jnp.zeros_like(rw)
    lax.fori_loop(0, tt // grp, token_group, 0)
    for k in range(grp):
        o_ref[tt - grp + k] = finish(tt - grp + k, rows[k])


def _v_side_constants(na):
    kk = np.arange(na * SUBLANES)
    rep = (np.arange(na)[:, None] == (2 * (kk // 16) + (kk % 16) // 8)[None, :]).astype(np.float32)
    keep = (np.arange(SUBLANES)[:, None] == (((kk % 16) // 2) % 4 + 4 * (kk % 2))[None, :]).astype(np.float32)
    return jnp.asarray(rep, BF16), jnp.asarray(keep, F32)


def _peer_experts(x, hmod, idx, gate, tbl_u, tbl_v, mod8, i_gate, *, seq):
    n, d = x.shape
    na = idx.shape[1]
    tt = min(PEER_TILE, seq)
    tpb = seq // tt
    x8 = x.reshape(n, SUBLANES, LANES)
    h8 = hmod.reshape(n, SUBLANES, LANES)
    tbl_spec = pl.BlockSpec(memory_space=pltpu.VMEM)
    idx_split = idx.reshape(n, na // IDX_SPLIT, IDX_SPLIT).transpose(2, 0, 1)
    idx_specs = [pl.BlockSpec((None, tt, na // IDX_SPLIT), functools.partial(lambda k, i: (k, i, 0), k),
                              memory_space=pltpu.SMEM) for k in range(IDX_SPLIT)]
    idx_args = [idx_split] * IDX_SPLIT
    w = pl.pallas_call(
        _peer_u_body,
        out_shape=jax.ShapeDtypeStruct((n, na), F32),
        grid=(n // tt,),
        in_specs=idx_specs + [
            tbl_spec,
            pl.BlockSpec((tt, SUBLANES, LANES), lambda i: (i, 0, 0)),
            pl.BlockSpec((tt, na), lambda i: (i, 0)),
        ],
        out_specs=pl.BlockSpec((tt, na), lambda i: (i, 0)),
        scratch_shapes=[pltpu.VMEM((na, LANES), F32)] * PEER_GROUP,
        compiler_params=_cparams(("arbitrary",)),
        name="peer_u",
    )(*idx_args, tbl_u, h8, gate)
    rep, keep = _v_side_constants(na)
    whole = lambda a: pl.BlockSpec(a.shape, lambda i: (0, 0))
    out = pl.pallas_call(
        _peer_v_body,
        out_shape=jax.ShapeDtypeStruct((n, SUBLANES, LANES), F32),
        grid=(n // tt,),
        in_specs=idx_specs + [
            tbl_spec,
            whole(rep), whole(keep),
            pl.BlockSpec((tt, na), lambda i: (i, 0)),
            pl.BlockSpec((tt, SUBLANES, LANES), lambda i: (i, 0, 0)),
            pl.BlockSpec((None, None, SUBLANES, LANES), lambda i: (i // tpb, i_gate, 0, 0)),
        ],
        out_specs=pl.BlockSpec((tt, SUBLANES, LANES), lambda i: (i, 0, 0)),
        scratch_shapes=[pltpu.VMEM((2 * tt, na * SUBLANES), F32)]
        + [pltpu.VMEM((na * SUBLANES // 2, LANES), I32)] * PEER_GROUP,
        compiler_params=_cparams(("arbitrary",)),
        name="peer_v",
    )(*idx_args, tbl_v, rep, keep, w, x8, mod8)
    return out.reshape(n, d)


def _peer_ffn(x, gain, mod, wq, sub_keys, u, v, *, seq):
    bsz = mod.shape[0]
    d = x.shape[1]
    hmod, idx, gate = _peer_route(x, gain, mod, 3, 4, wq.astype(BF16), sub_keys.astype(BF16), seq=seq)
    mod8 = mod.reshape(bsz, mod.shape[1], SUBLANES, LANES)
    return _peer_experts(x, hmod, idx, gate, _pack_table(u), _pack_table(v), mod8, 5, seq=seq)


def kernel(x, c, ada_w, ada_b, norm_mix_g, norm_ffn_g, ma_w_in, ma_conv_w, ma_b_if, ma_hnorm_g, ma_w_out,
           kv_ada_w, kv_ada_b, kv_norm_g, kv_w, k_norm_g, sb_w_q, sb_q_norm_g, sb_w_out,
           peer_w_q, peer_sub_keys, peer_u, peer_v):
    bsz, seq, d = x.shape
    n = bsz * seq
    depth = ada_w.shape[0]
    n_a = ma_w_in.shape[0]
    sb_dim = k_norm_g.shape[0]
    sb_heads = d // sb_dim

    mods = _ada(c, ada_w, ada_b).reshape(depth, bsz, 6, 1, d)
    kv_mod = _ada(c, kv_ada_w[None], kv_ada_b[None]).reshape(bsz, 2, 1, d)
    xs = x.reshape(n, d)
    k_sh = v_sh = None
    for layer in range(depth):
        mod = mods[layer]
        g_mix = norm_mix_g[layer].reshape(1, d)
        if layer < n_a:
            w_in = ma_w_in[layer]
            pad = (-w_in.shape[1]) % LANES
            w_in = jnp.pad(w_in, ((0, 0), (0, pad))).astype(BF16)
            ncols = w_in.shape[1]
            tn = ncols // 5 if ncols % (5 * LANES) == 0 else None
            proj = _nmm(xs, w_in, seq=seq, norm=(g_mix, mod, 0, 1), tn=tn, name="mlstm_proj")
            hh = _mlstm(proj, ma_conv_w[layer], ma_b_if[layer], ma_hnorm_g[layer], bsz=bsz, seq=seq, d=d)
            xs = _nmm(hh, ma_w_out[layer].astype(BF16), seq=seq, resid=(xs, mod, 2), name="mlstm_out")
        else:
            if layer == n_a:
                kvw = kv_w.astype(BF16)
                k_gain = jnp.tile(k_norm_g, sb_heads).reshape(1, d)
                g_kv = kv_norm_g.reshape(1, d)
                k_sh = _nmm(xs, kvw[:, :d], seq=seq, norm=(g_kv, kv_mod, 0, 1),
                            headnorm=(k_gain, sb_dim), out_dtype=BF16, name="kv_k")
                v_sh = _nmm(xs, kvw[:, d:], seq=seq, norm=(g_kv, kv_mod, 0, 1), out_dtype=BF16, name="kv_v")
            j = layer - n_a
            q_gain = (jnp.tile(sb_q_norm_g[j], sb_heads) * (sb_dim ** -0.5 * LOG2E)).reshape(1, d)
            q = _nmm(xs, sb_w_q[j].astype(BF16), seq=seq, norm=(g_mix, mod, 0, 1),
                     headnorm=(q_gain, sb_dim), out_dtype=BF16, name="sb_q")
            o = _sb_attention(q, k_sh, v_sh, bsz=bsz, seq=seq, head_dim=sb_dim)
            xs = _nmm(o, sb_w_out[j].astype(BF16), seq=seq, resid=(xs, mod, 2), name="sb_out")
        xs = _peer_ffn(xs, norm_ffn_g[layer].reshape(1, d), mod, peer_w_q[layer], peer_sub_keys[layer],
                       peer_u[layer], peer_v[layer], seq=seq)
    return xs.reshape(bsz, seq, d)
```

```python
import functools

import jax
import jax.numpy as jnp
import numpy as np
from jax import lax
from jax.experimental import pallas as pl
from jax.experimental.pallas import tpu as pltpu

F32 = jnp.float32
BF16 = jnp.bfloat16
I32 = jnp.int32

EPS = 1e-6
PEER_TOPK = 16
LANES = 128
SUBLANES = 8
VMEM_LIMIT_BYTES = 56 * 1024 * 1024

MLSTM_CHUNK = 256
SB_TILE_Q = 256
SB_TILE_K = 128
SB_LANES = 512
LOG2E = 1.4426950408889634
MM_TILE_M = 512
ROUTE_TILE = 256
ROUTE_SUB = 128
PEER_TILE = 64
PEER_GROUP = 8
IDX_SPLIT = 8
BITREV8 = (0, 4, 2, 6, 1, 5, 3, 7)


def _cparams(sem):
    return pltpu.CompilerParams(dimension_semantics=sem, vmem_limit_bytes=VMEM_LIMIT_BYTES)


def _dot(a, b):
    return jnp.dot(a, b, preferred_element_type=F32)


def _dot_tb(a, b):
    return lax.dot_general(a, b, (((1,), (1,)), ((), ())), preferred_element_type=F32)


def _split_dot(x, m):
    hi = x.astype(BF16)
    lo = (x - hi.astype(F32)).astype(BF16)
    return _dot(hi, m) + _dot(lo, m)


def _sigmoid(x):
    return 1.0 / (1.0 + jnp.exp(-x))


def _softplus(x):
    return jnp.maximum(x, 0.0) + jnp.log1p(jnp.exp(-jnp.abs(x)))


def _rmsnorm_mod(x, g, shift, scale):
    ms = jnp.mean(x * x, axis=-1, keepdims=True)
    y = x * lax.rsqrt(ms + EPS) * g
    return y * (1.0 + scale) + shift


def _ada_body(c_ref, w_ref, b_ref, o_ref):
    c = c_ref[...]
    a = (c * _sigmoid(c)).astype(BF16)
    o_ref[...] = _dot(a, w_ref[...].astype(BF16)) + b_ref[...]


def _ada(c, w, b):
    nl, d, e = w.shape
    bsz = c.shape[0]
    te = 1024
    return pl.pallas_call(
        _ada_body,
        out_shape=jax.ShapeDtypeStruct((nl, bsz, e), F32),
        grid=(nl, e // te),
        in_specs=[
            pl.BlockSpec((bsz, d), lambda l, j: (0, 0)),
            pl.BlockSpec((None, d, te), lambda l, j: (l, 0, j)),
            pl.BlockSpec((None, 1, te), lambda l, j: (l, 0, j)),
        ],
        out_specs=pl.BlockSpec((None, bsz, te), lambda l, j: (l, 0, j)),
        compiler_params=_cparams(("arbitrary", "arbitrary")),
        name="ada_mod",
    )(c, w, b.reshape(nl, 1, e))


def _nmm_body(*refs, prologue, epilogue, head_dim):
    it = iter(refs)
    x_ref = next(it)
    if prologue:
        g_ref, sh_ref, sc_ref = next(it), next(it), next(it)
    w_ref = next(it)
    if epilogue == "resid":
        res_ref, gate_ref = next(it), next(it)
    if epilogue == "headnorm":
        hg_ref = next(it)
    o_ref = next(it)
    h_ref = next(it)

    @pl.when(pl.program_id(1) == 0)
    def _():
        x = x_ref[...]
        if prologue:
            x = _rmsnorm_mod(x, g_ref[...], sh_ref[...], sc_ref[...])
        h_ref[...] = x.astype(BF16)

    acc = _dot(h_ref[...], w_ref[...])
    if epilogue == "resid":
        acc = res_ref[...] + gate_ref[...] * acc
    elif epilogue == "headnorm":
        r = lax.broadcasted_iota(I32, (LANES, LANES), 0) // head_dim
        c = lax.broadcasted_iota(I32, (LANES, LANES), 1) // head_dim
        group = jnp.where(r == c, 1.0, 0.0).astype(BF16)
        sq = acc * acc
        ms = jnp.concatenate(
            [_split_dot(sq[:, j * LANES:(j + 1) * LANES], group) for j in range(acc.shape[1] // LANES)],
            axis=1) * (1.0 / head_dim)
        acc = acc * lax.rsqrt(ms + EPS) * hg_ref[...]
    o_ref[...] = acc.astype(o_ref.dtype)


def _nmm(x, w, *, seq, norm=None, resid=None, headnorm=None, out_dtype=F32, tn=None, name):
    n, d = x.shape
    e = w.shape[1]
    tm = min(MM_TILE_M, seq)
    tn = e if tn is None else tn
    tpb = seq // tm
    args = [x]
    in_specs = [pl.BlockSpec((tm, d), lambda i, j: (i, 0))]
    if norm is not None:
        gain, mod, i_sh, i_sc = norm
        args += [gain, mod, mod]
        in_specs += [
            pl.BlockSpec((1, d), lambda i, j: (0, 0)),
            pl.BlockSpec((None, None, 1, d), lambda i, j: (i // tpb, i_sh, 0, 0)),
            pl.BlockSpec((None, None, 1, d), lambda i, j: (i // tpb, i_sc, 0, 0)),
        ]
    args.append(w)
    in_specs.append(pl.BlockSpec((d, tn), lambda i, j: (0, j)))
    epilogue = None
    head_dim = 0
    if resid is not None:
        res, mod, i_g = resid
        epilogue = "resid"
        args += [res, mod]
        in_specs += [
            pl.BlockSpec((tm, tn), lambda i, j: (i, j)),
            pl.BlockSpec((None, None, 1, tn), lambda i, j: (i // tpb, i_g, 0, j)),
        ]
    if headnorm is not None:
        hg, head_dim = headnorm
        epilogue = "headnorm"
        args.append(hg)
        in_specs.append(pl.BlockSpec((1, tn), lambda i, j: (0, j)))
    return pl.pallas_call(
        functools.partial(_nmm_body, prologue=norm is not None, epilogue=epilogue, head_dim=head_dim),
        out_shape=jax.ShapeDtypeStruct((n, e), out_dtype),
        grid=(n // tm, e // tn),
        in_specs=in_specs,
        out_specs=pl.BlockSpec((tm, tn), lambda i, j: (i, j)),
        scratch_shapes=[pltpu.VMEM((tm, d), BF16)],
        compiler_params=_cparams(("arbitrary", "arbitrary")),
        name=name,
    )(*args)


def _mlstm_body(qk_ref, v_ref, o_ref, gt_ref, cw_ref, bif_ref, hg_ref, out_ref,
                tail_ref, c_ref, m_ref, *, heads, qk_dim, conv_w):
    L = qk_ref.shape[0]
    nqk = heads * qk_dim
    vdim = LANES

    @pl.when(pl.program_id(1) == 0)
    def _():
        tail_ref[...] = jnp.zeros_like(tail_ref)
        c_ref[...] = jnp.zeros_like(c_ref)
        m_ref[...] = jnp.zeros_like(m_ref)

    x = qk_ref[...]
    tail = tail_ref[...]
    rows8 = lax.broadcasted_iota(I32, (SUBLANES, x.shape[1]), 0)
    acc = x * cw_ref[conv_w - 1:conv_w, :]
    for j in range(1, conv_w):
        xs = pltpu.roll(x, j, axis=0)
        head8 = jnp.where(rows8 < j, pltpu.roll(tail, j, axis=0), xs[0:SUBLANES])
        xs = jnp.concatenate([head8, xs[SUBLANES:]], axis=0)
        acc = acc + xs * cw_ref[conv_w - 1 - j:conv_w - j, :]
    tail_ref[...] = x[L - SUBLANES:, :]
    qk = acc * _sigmoid(acc)
    q_all = qk[:, :nqk] * (qk_dim ** -0.5)
    k_all = qk[:, nqk:]
    kt_all = k_all.T.astype(BF16)

    gates = gt_ref[...] + bif_ref[...]
    log_f = -_softplus(-gates)
    r = lax.broadcasted_iota(I32, (L, L), 0)
    c = lax.broadcasted_iota(I32, (L, L), 1)
    causal = r >= c
    tri = jnp.where(causal, 1.0, 0.0).astype(BF16)
    lf_hi = log_f.astype(BF16)
    lf_lo = (log_f - lf_hi.astype(F32)).astype(BF16)
    bcum = _dot(tri, lf_hi) + _dot(tri, lf_lo)
    gates_t = gates.T
    bcum_t = bcum.T
    lane = lax.broadcasted_iota(I32, (L, LANES), 1)
    ones_v = jnp.ones((L, vdim), BF16)
    rows_c = lax.broadcasted_iota(I32, (LANES, 2 * vdim), 0)

    for h in range(heads):
        pair, half = h // 2, h % 2
        b_col = bcum[:, heads + h:heads + h + 1]
        i_col = gates[:, h:h + 1]
        b_row = bcum_t[heads + h:heads + h + 1, :]
        i_row = gates_t[h:h + 1, :]
        b_last = b_col[L - 1:L, :]
        m_prev = m_ref[h:h + 1, 0:1]
        in_head = (lane >= half * qk_dim) & (lane < (half + 1) * qk_dim)
        q_h = jnp.where(in_head, q_all[:, pair * LANES:(pair + 1) * LANES], 0.0).astype(BF16)
        kt_h = kt_all[pair * LANES:(pair + 1) * LANES, :]
        v_h = v_ref[:, h * vdim:(h + 1) * vdim]
        v_aug = jnp.concatenate([v_h.astype(BF16), ones_v], axis=1)

        log_d = jnp.where(causal, b_col - b_row + i_row, -jnp.inf)
        inter_log = b_col + m_prev
        m_t = jnp.maximum(inter_log, jnp.max(log_d, axis=1, keepdims=True))
        inter_w = jnp.exp(inter_log - m_t)
        s = _dot(q_h, kt_h) * jnp.exp(log_d - m_t)
        c_prev = c_ref[h]
        tot = _dot(s.astype(BF16), v_aug) + inter_w * _dot(q_h, c_prev.astype(BF16))
        num = tot[:, :vdim]
        den = tot[:, vdim:]
        hh = num / jnp.maximum(jnp.abs(den), jnp.exp(-m_t))
        ms = jnp.mean(hh * hh, axis=1, keepdims=True)
        hh = hh * lax.rsqrt(ms + EPS) * hg_ref[:, h * vdim:(h + 1) * vdim]
        hh = hh * _sigmoid(o_ref[:, h * vdim:(h + 1) * vdim])
        out_ref[:, h * vdim:(h + 1) * vdim] = hh.astype(out_ref.dtype)

        w_state = b_last - b_col + i_col
        m_loc = jnp.max(w_state, axis=0, keepdims=True)
        e_state = jnp.exp(w_state - m_loc)
        ev = (e_state * jnp.concatenate([v_h, jnp.ones((L, vdim), F32)], axis=1)).astype(BF16)
        c_loc = _dot(kt_h, ev)
        in_rows = (rows_c >= half * qk_dim) & (rows_c < (half + 1) * qk_dim)
        c_loc = jnp.where(in_rows, c_loc, 0.0)
        m_new = jnp.maximum(b_last + m_prev, m_loc)
        a = jnp.exp(b_last + m_prev - m_new)
        rr = jnp.exp(m_loc - m_new)
        c_ref[h] = a * c_prev + rr * c_loc
        m_ref[h:h + 1, :] = jnp.broadcast_to(m_new, (1, LANES))


def _mlstm(proj, conv_w, b_if, hnorm_g, *, bsz, seq, d):
    heads, vdim = hnorm_g.shape
    assert vdim == LANES
    nqk2 = conv_w.shape[1]
    qk_dim = nqk2 // (2 * heads)
    width = conv_w.shape[0]
    L = min(MLSTM_CHUNK, seq)
    nc = seq // L
    bif = jnp.zeros((1, LANES), F32).at[0, :2 * heads].set(b_if)
    gate_blk = (nqk2 + 2 * d) // LANES
    return pl.pallas_call(
        functools.partial(_mlstm_body, heads=heads, qk_dim=qk_dim, conv_w=width),
        out_shape=jax.ShapeDtypeStruct((bsz * seq, d), BF16),
        grid=(bsz, nc),
        in_specs=[
            pl.BlockSpec((L, nqk2), lambda b, c: (b * nc + c, 0)),
            pl.BlockSpec((L, d), lambda b, c: (b * nc + c, nqk2 // d)),
            pl.BlockSpec((L, d), lambda b, c: (b * nc + c, nqk2 // d + 1)),
            pl.BlockSpec((L, LANES), lambda b, c: (b * nc + c, gate_blk)),
            pl.BlockSpec((width, nqk2), lambda b, c: (0, 0)),
            pl.BlockSpec((1, LANES), lambda b, c: (0, 0)),
            pl.BlockSpec((1, d), lambda b, c: (0, 0)),
        ],
        out_specs=pl.BlockSpec((L, d), lambda b, c: (b * nc + c, 0)),
        scratch_shapes=[
            pltpu.VMEM((SUBLANES, nqk2), F32),
            pltpu.VMEM((heads, LANES, 2 * LANES), F32),
            pltpu.VMEM((heads, LANES), F32),
        ],
        compiler_params=_cparams(("arbitrary", "arbitrary")),
        name="mlstm",
    )(proj, proj, proj, proj, conv_w, bif, hnorm_g.reshape(1, d))


def _sb_body(q_ref, k_ref, v_ref, o_ref, *, head_dim):
    TQ = q_ref.shape[0]
    TK = SB_TILE_K
    kpq = TQ // TK
    npair = q_ref.shape[1] // LANES
    nh = 2 * npair
    qi = pl.program_id(2)
    lane = lax.broadcasted_iota(I32, (TQ, LANES), 1)
    first = lane < head_dim
    q_heads = []
    for p in range(npair):
        q = q_ref[:, p * LANES:(p + 1) * LANES].astype(F32)
        q_heads += [jnp.where(first, q, 0.0).astype(BF16), jnp.where(first, 0.0, q).astype(BF16)]
    r = lax.broadcasted_iota(I32, (TK, TK), 0)
    c = lax.broadcasted_iota(I32, (TK, TK), 1)
    later = jnp.where(r >= c, 1.0, 0.0).astype(BF16)
    row = lax.broadcasted_iota(I32, (TQ, TK), 0)
    col = lax.broadcasted_iota(I32, (TQ, TK), 1)

    def block(kb, carry, dk):
        off = pl.multiple_of(kb * TK, TK)
        strict = None if dk is None else (col + dk * TK) < row
        kblk = [k_ref[pl.ds(off, TK), p * LANES:(p + 1) * LANES] for p in range(npair)]
        vblk = [v_ref[pl.ds(off, TK), p * LANES:(p + 1) * LANES] for p in range(npair)]
        zs = [_dot_tb(q_heads[h], kblk[h // 2]) for h in range(nh)]
        ls = []
        for z in zs:
            l = jnp.maximum(z, 0.0) + jnp.log2(1.0 + jnp.exp2(-jnp.abs(z)))
            ls.append(l if strict is None else jnp.where(strict, l, 0.0))
        tots = [_split_dot(ls[h], later) + carry[2 * h + 1] for h in range(nh)]
        new = []
        for h in range(nh):
            a = jnp.exp2(zs[h] - tots[h])
            if strict is not None:
                a = jnp.where(strict, a, 0.0)
            new += [carry[2 * h] + _dot(a.astype(BF16), vblk[h // 2]),
                    jnp.broadcast_to(tots[h][:, 0:1], (TQ, TK))]
        return tuple(new)

    carry = []
    for _ in range(nh):
        carry += [jnp.zeros((TQ, LANES), F32), jnp.zeros((TQ, TK), F32)]
    carry = tuple(carry)
    for dk in reversed(range(kpq)):
        carry = block(qi * kpq + dk, carry, dk)
    carry = lax.fori_loop(0, qi * kpq, lambda j, cr: block(qi * kpq - 1 - j, cr, None), carry)
    for p in range(npair):
        o_ref[:, p * LANES:(p + 1) * LANES] = jnp.where(first, carry[4 * p], carry[4 * p + 2]).astype(o_ref.dtype)


def _sb_attention(q, k, v, *, bsz, seq, head_dim):
    n, d = q.shape
    assert 2 * head_dim == LANES
    t = min(SB_TILE_Q, seq)
    assert t % SB_TILE_K == 0
    w = min(SB_LANES, d)
    q3, k3, v3 = (a.reshape(bsz, seq, d) for a in (q, k, v))
    out = pl.pallas_call(
        functools.partial(_sb_body, head_dim=head_dim),
        out_shape=jax.ShapeDtypeStruct((bsz, seq, d), BF16),
        grid=(bsz, d // w, seq // t),
        in_specs=[
            pl.BlockSpec((None, t, w), lambda b, hp, i: (b, i, hp)),
            pl.BlockSpec((None, seq, w), lambda b, hp, i: (b, 0, hp)),
            pl.BlockSpec((None, seq, w), lambda b, hp, i: (b, 0, hp)),
        ],
        out_specs=pl.BlockSpec((None, t, w), lambda b, hp, i: (b, i, hp)),
        compiler_params=_cparams(("arbitrary", "arbitrary", "arbitrary")),
        name="sb_attention",
    )(q3, k3, v3)
    return out.reshape(n, d)


def _topk_rows(s, k, rows=None, big=None):
    if rows is None:
        rows = lax.broadcasted_iota(I32, s.shape, 0).astype(F32)
        big = float(s.shape[0])
    vals, idxs = [], []
    for _ in range(k):
        m = jnp.max(s, axis=0, keepdims=True)
        i = jnp.min(jnp.where(s == m, rows, big), axis=0, keepdims=True)
        vals.append(m)
        idxs.append(i)
        s = jnp.where(rows == i, -jnp.inf, s)
    return jnp.concatenate(vals, axis=0), jnp.concatenate(idxs, axis=0)


def _select_rows(table, sel, k):
    out = jnp.zeros_like(sel)
    for a in range(k):
        out = jnp.where(sel == float(a), table[a:a + 1, :], out)
    return out


def _route_body(x_ref, g_ref, sh_ref, sc_ref, wq_ref, keys_ref, h_ref, idx_ref, gate_ref,
                q_scr, idx_t, gate_t, *, heads, nkeys):
    k = PEER_TOPK
    tm = x_ref.shape[0]
    hmod = _rmsnorm_mod(x_ref[...], g_ref[...], sh_ref[...], sc_ref[...])
    h_ref[...] = hmod
    q_scr[...] = _dot(hmod.astype(BF16), wq_ref[...]).astype(BF16)
    nsub = tm // ROUTE_SUB
    ncand = sum(k // (a + 1) for a in range(k))
    cand_pad = (-ncand) % SUBLANES
    ridx = lax.broadcasted_iota(I32, (ncand + cand_pad, ROUTE_SUB), 0)
    cand_pos = jnp.full(ridx.shape, k * k, I32)
    start = 0
    for a in range(k):
        nb = k // (a + 1)
        cand_pos = jnp.where((ridx >= start) & (ridx < start + nb), ridx + (a * k - start), cand_pos)
        start += nb
    cand_pos = cand_pos.astype(F32)

    def head_step(it, carry):
        hd = it // nsub
        sub = it % nsub
        tok0 = pl.multiple_of(sub * ROUTE_SUB, ROUTE_SUB)
        tops = []
        for p in range(2):
            c0 = pl.multiple_of((hd * 2 + p) * LANES, LANES)
            qs = q_scr[pl.ds(tok0, ROUTE_SUB), pl.ds(c0, LANES)]
            st = _dot_tb(keys_ref[p], qs)
            tops.append(_topk_rows(st, k))
        (v0, i0), (v1, i1) = tops
        pieces = [v0[a:a + 1, :] + v1[0:k // (a + 1), :] for a in range(k)]
        pieces.append(jnp.full((cand_pad, ROUTE_SUB), -jnp.inf, F32))
        cand = jnp.concatenate(pieces, axis=0)
        g_top, pos = _topk_rows(cand, k, rows=cand_pos, big=float(k * k))
        pa = jnp.floor(pos * (1.0 / k))
        pb = pos - pa * k
        expert = (_select_rows(i0, pa, k) * float(nkeys) + _select_rows(i1, pb, k)) * float(SUBLANES // 2)
        e = jnp.exp(g_top - g_top[0:1, :])
        gate = e / jnp.sum(e, axis=0, keepdims=True)
        r0 = pl.multiple_of(hd * k, k)
        idx_t[pl.ds(r0, k), pl.ds(tok0, ROUTE_SUB)] = expert
        gate_t[pl.ds(r0, k), pl.ds(tok0, ROUTE_SUB)] = gate
        return carry

    lax.fori_loop(0, heads * nsub, head_step, 0)
    idx_ref[...] = idx_t[...].T.astype(I32)
    gate_ref[...] = gate_t[...].T


def _peer_route(x, gain, mod, i_sh, i_sc, wq, keys, *, seq):
    n, d = x.shape
    qcols = wq.shape[1]
    nkeys, kdim = keys.shape[1], keys.shape[2]
    assert kdim == LANES and nkeys == LANES
    heads = qcols // (2 * kdim)
    tm = min(ROUTE_TILE, seq)
    tpb = seq // tm
    ha = heads * PEER_TOPK
    return pl.pallas_call(
        functools.partial(_route_body, heads=heads, nkeys=nkeys),
        out_shape=(jax.ShapeDtypeStruct((n, d), F32),
                   jax.ShapeDtypeStruct((n, ha), I32),
                   jax.ShapeDtypeStruct((n, ha), F32)),
        grid=(n // tm,),
        in_specs=[
            pl.BlockSpec((tm, d), lambda i: (i, 0)),
            pl.BlockSpec((1, d), lambda i: (0, 0)),
            pl.BlockSpec((None, None, 1, d), lambda i: (i // tpb, i_sh, 0, 0)),
            pl.BlockSpec((None, None, 1, d), lambda i: (i // tpb, i_sc, 0, 0)),
            pl.BlockSpec((d, qcols), lambda i: (0, 0)),
            pl.BlockSpec((2, nkeys, kdim), lambda i: (0, 0, 0)),
        ],
        out_specs=(pl.BlockSpec((tm, d), lambda i: (i, 0)),
                   pl.BlockSpec((tm, ha), lambda i: (i, 0)),
                   pl.BlockSpec((tm, ha), lambda i: (i, 0))),
        scratch_shapes=[pltpu.VMEM((tm, qcols), BF16),
                        pltpu.VMEM((ha, tm), F32),
                        pltpu.VMEM((ha, tm), F32)],
        compiler_params=_cparams(("arbitrary",)),
        name="peer_route",
    )(x, gain, mod, mod, wq, keys)


def _pack_table(t):
    e, d = t.shape
    assert d == SUBLANES * LANES
    half = SUBLANES // 2
    bits = lax.bitcast_convert_type(t.astype(BF16).reshape(e, 2, half, LANES), jnp.uint16).astype(jnp.uint32)
    words = lax.bitcast_convert_type(bits[:, 0] | (bits[:, 1] << 16), I32).reshape(e * half, LANES)
    return jnp.pad(words, ((half, half), (0, 0)))


def _row_of(idx_refs, t, a):
    return idx_refs[a % IDX_SPLIT][t, a // IDX_SPLIT]


def _window_pair(tbl_ref, row_a, row_b, low_half):
    half = SUBLANES // 2
    wa = tbl_ref[pl.ds(row_a + half, SUBLANES), :]
    wb = tbl_ref[pl.ds(row_b, SUBLANES), :]
    return jnp.where(low_half, wa, wb)


def _unpack(w):
    lo = lax.bitcast_convert_type(w << 16, F32)
    hi = lax.bitcast_convert_type(w & jnp.int32(-65536), F32)
    return lo, hi


def _merge(v1, v2, g, first_half):
    h = g // 2
    a = v1 + pltpu.roll(v1, SUBLANES - h, axis=0)
    b = v2 + pltpu.roll(v2, h, axis=0)
    return jnp.where(first_half, a, b)


def _erf(x):
    ax = jnp.abs(x)
    t = 1.0 / (1.0 + 0.3275911 * ax)
    poly = ((((1.061405429 * t - 1.453152027) * t + 1.421413741) * t - 0.284496736) * t + 0.254829592) * t
    y = 1.0 - poly * jnp.exp(-ax * ax)
    return jnp.where(x < 0, -y, y)


def _peer_u_body(*refs):
    idx_refs = refs[:IDX_SPLIT]
    tbl_ref, x_ref, g_ref, w_ref = refs[IDX_SPLIT:IDX_SPLIT + 4]
    parts = refs[IDX_SPLIT + 4:]
    tt, na = g_ref.shape
    half = SUBLANES // 2
    sub = lax.broadcasted_iota(I32, (SUBLANES, LANES), 0)
    low_half = sub < half
    first4 = (sub & 3) < 2
    first2 = (sub & 1) < 1
    sub4 = [(sub & 3) == k for k in range(3)]

    def finish(t, part_ref):
        act = jnp.sum(part_ref[...].T, axis=0, keepdims=True)
        gelu = 0.5 * act * (1.0 + _erf(act * (2.0 ** -0.5)))
        return g_ref[pl.ds(t, 1), :] * gelu

    def gather(t, part_ref):
        xrow = x_ref[pl.ds(t, 1), :]
        xc = [jnp.broadcast_to(xrow[:, c * LANES:(c + 1) * LANES], (SUBLANES, LANES)) for c in range(SUBLANES)]
        xl = jnp.where(sub4[0], xc[0], jnp.where(sub4[1], xc[1], jnp.where(sub4[2], xc[2], xc[3])))
        xh = jnp.where(sub4[0], xc[4], jnp.where(sub4[1], xc[5], jnp.where(sub4[2], xc[6], xc[7])))
        groups = []
        for grp in range(na // SUBLANES):
            c = []
            for i in range(half):
                a0 = grp * SUBLANES + BITREV8[2 * i]
                a1 = grp * SUBLANES + BITREV8[2 * i + 1]
                lo, hi = _unpack(_window_pair(tbl_ref, _row_of(idx_refs, t, a0), _row_of(idx_refs, t, a1),
                                              low_half))
                c.append(lo * xl + hi * xh)
            d0 = _merge(c[0], c[1], 4, first4)
            d1 = _merge(c[2], c[3], 4, first4)
            groups.append(_merge(d0, d1, 2, first2))
        part_ref[...] = jnp.concatenate(groups, axis=0)

    grp = len(parts)

    def token_group(i, carry):
        t = grp * i
        done = [finish(t - grp + k, parts[k]) for k in range(grp)]
        for k in range(grp):
            gather(t + k, parts[k])
        for k in range(grp):
            w_ref[pl.ds(t - grp + k, 1), :] = done[k]
        return carry

    for k in range(grp):
        gather(k, parts[k])
    lax.fori_loop(1, tt // grp, token_group, 0)
    for k in range(grp):
        w_ref[pl.ds(tt - grp + k, 1), :] = finish(tt - grp + k, parts[k])


def _peer_v_body(*refs):
    idx_refs = refs[:IDX_SPLIT]
    tbl_ref, rep_ref, keep_ref, w_ref, res_ref, gate_ref, o_ref, spread_ref = refs[IDX_SPLIT:IDX_SPLIT + 8]
    rows = refs[IDX_SPLIT + 8:]
    tt = res_ref.shape[0]
    na = w_ref.shape[1]
    half = SUBLANES // 2
    low_half = lax.broadcasted_iota(I32, (SUBLANES, LANES), 0) < half

    w_all = w_ref[...]
    w_hi = w_all.astype(BF16)
    w_lo = (w_all - w_hi.astype(F32)).astype(BF16)
    spread_ref[...] = _dot(jnp.concatenate([w_hi, w_lo], axis=0), rep_ref[...])
    keep = keep_ref[...]

    def gather(t, rows_ref):
        for j in range(na // 2):
            tile = _window_pair(tbl_ref, _row_of(idx_refs, t, 2 * j), _row_of(idx_refs, t, 2 * j + 1), low_half)
            rows_ref[j * SUBLANES:(j + 1) * SUBLANES, :] = tile

    def finish(t, rows_ref):
        lhs = jnp.concatenate([spread_ref[pl.ds(t, 1), :] * keep, spread_ref[pl.ds(tt + t, 1), :] * keep],
                              axis=0).astype(BF16)
        out = _dot(lhs, pltpu.bitcast(rows_ref[...], BF16))
        y = out[0:SUBLANES] + out[SUBLANES:]
        yrow = jnp.concatenate([y[c:c + 1, :] for c in range(SUBLANES)], axis=1)
        return res_ref[pl.ds(t, 1), :] + gate_ref[...] * yrow

    def store(t, row):
        o_ref[pl.ds(t, 1), :] = row

    grp = len(rows)

    def token_group(i, carry):
        t = grp * i
        done = [finish(t - grp + k, rows[k]) for k in range(grp)]
        for k in range(grp):
            gather(t + k, rows[k])
        for k in range(grp):
            store(t - grp + k, done[k])
        return carry

    for k in range(grp):
        gather(k, rows[k])
    lax.fori_loop(1, tt // grp, token_group, 0)
    for k in range(grp):
        store(tt - grp + k, finish(tt - grp + k, rows[k]))


def _v_side_constants(na):
    kk = np.arange(na * SUBLANES)
    rep = (np.arange(na)[:, None] == (2 * (kk // 16) + (kk % 16) // 8)[None, :]).astype(np.float32)
    keep = (np.arange(SUBLANES)[:, None] == (((kk % 16) // 2) % 4 + 4 * (kk % 2))[None, :]).astype(np.float32)
    return jnp.asarray(rep, BF16), jnp.asarray(keep, F32)


def _peer_experts(x, hmod, idx, gate, tbl_u, tbl_v, mod, i_gate, *, seq):
    n, d = x.shape
    na = idx.shape[1]
    tt = min(PEER_TILE, seq)
    tpb = seq // tt
    tbl_spec = pl.BlockSpec(memory_space=pltpu.VMEM)
    idx_split = idx.reshape(n, na // IDX_SPLIT, IDX_SPLIT).transpose(2, 0, 1)
    idx_specs = [pl.BlockSpec((None, tt, na // IDX_SPLIT), functools.partial(lambda k, i: (k, i, 0), k),
                              memory_space=pltpu.SMEM) for k in range(IDX_SPLIT)]
    idx_args = [idx_split] * IDX_SPLIT
    w = pl.pallas_call(
        _peer_u_body,
        out_shape=jax.ShapeDtypeStruct((n, na), F32),
        grid=(n // tt,),
        in_specs=idx_specs + [
            tbl_spec,
            pl.BlockSpec((tt, d), lambda i: (i, 0)),
            pl.BlockSpec((tt, na), lambda i: (i, 0)),
        ],
        out_specs=pl.BlockSpec((tt, na), lambda i: (i, 0)),
        scratch_shapes=[pltpu.VMEM((na, LANES), F32)] * PEER_GROUP,
        compiler_params=_cparams(("arbitrary",)),
        name="peer_u",
    )(*idx_args, tbl_u, hmod, gate)
    rep, keep = _v_side_constants(na)
    whole = lambda a: pl.BlockSpec(a.shape, lambda i: (0, 0))
    out = pl.pallas_call(
        _peer_v_body,
        out_shape=jax.ShapeDtypeStruct((n, d), F32),
        grid=(n // tt,),
        in_specs=idx_specs + [
            tbl_spec,
            whole(rep), whole(keep),
            pl.BlockSpec((tt, na), lambda i: (i, 0)),
            pl.BlockSpec((tt, d), lambda i: (i, 0)),
            pl.BlockSpec((None, None, 1, d), lambda i: (i // tpb, i_gate, 0, 0)),
        ],
        out_specs=pl.BlockSpec((tt, d), lambda i: (i, 0)),
        scratch_shapes=[pltpu.VMEM((2 * tt, na * SUBLANES), F32)]
        + [pltpu.VMEM((na * SUBLANES // 2, LANES), I32)] * PEER_GROUP,
        compiler_params=_cparams(("arbitrary",)),
        name="peer_v",
    )(*idx_args, tbl_v, rep, keep, w, x, mod)
    return out


def _peer_ffn(x, gain, mod, wq, sub_keys, u, v, *, seq):
    hmod, idx, gate = _peer_route(x, gain, mod, 3, 4, wq.astype(BF16), sub_keys.astype(BF16), seq=seq)
    return _peer_experts(x, hmod, idx, gate, _pack_table(u), _pack_table(v), mod, 5, seq=seq)


def kernel(x, c, ada_w, ada_b, norm_mix_g, norm_ffn_g, ma_w_in, ma_conv_w, ma_b_if, ma_hnorm_g, ma_w_out,
           kv_ada_w, kv_ada_b, kv_norm_g, kv_w, k_norm_g, sb_w_q, sb_q_norm_g, sb_w_out,
           peer_w_q, peer_sub_keys, peer_u, peer_v):
    bsz, seq, d = x.shape
    n = bsz * seq
    depth = ada_w.shape[0]
    n_a = ma_w_in.shape[0]
    sb_dim = k_norm_g.shape[0]
    sb_heads = d // sb_dim

    mods = _ada(c, ada_w, ada_b).reshape(depth, bsz, 6, 1, d)
    kv_mod = _ada(c, kv_ada_w[None], kv_ada_b[None]).reshape(bsz, 2, 1, d)
    xs = x.reshape(n, d)
    k_sh = v_sh = None
    for layer in range(depth):
        mod = mods[layer]
        g_mix = norm_mix_g[layer].reshape(1, d)
        if layer < n_a:
            w_in = ma_w_in[layer]
            pad = (-w_in.shape[1]) % LANES
            w_in = jnp.pad(w_in, ((0, 0), (0, pad))).astype(BF16)
            ncols = w_in.shape[1]
            tn = ncols // 5 if ncols % (5 * LANES) == 0 else None
            proj = _nmm(xs, w_in, seq=seq, norm=(g_mix, mod, 0, 1), tn=tn, name="mlstm_proj")
            hh = _mlstm(proj, ma_conv_w[layer], ma_b_if[layer], ma_hnorm_g[layer], bsz=bsz, seq=seq, d=d)
            xs = _nmm(hh, ma_w_out[layer].astype(BF16), seq=seq, resid=(xs, mod, 2), name="mlstm_out")
        else:
            if layer == n_a:
                kvw = kv_w.astype(BF16)
                k_gain = jnp.tile(k_norm_g, sb_heads).reshape(1, d)
                g_kv = kv_norm_g.reshape(1, d)
                k_sh = _nmm(xs, kvw[:, :d], seq=seq, norm=(g_kv, kv_mod, 0, 1),
                            headnorm=(k_gain, sb_dim), out_dtype=BF16, name="kv_k")
                v_sh = _nmm(xs, kvw[:, d:], seq=seq, norm=(g_kv, kv_mod, 0, 1), out_dtype=BF16, name="kv_v")
            j = layer - n_a
            q_gain = (jnp.tile(sb_q_norm_g[j], sb_heads) * (sb_dim ** -0.5 * LOG2E)).reshape(1, d)
            q = _nmm(xs, sb_w_q[j].astype(BF16), seq=seq, norm=(g_mix, mod, 0, 1),
                     headnorm=(q_gain, sb_dim), out_dtype=BF16, name="sb_q")
            o = _sb_attention(q, k_sh, v_sh, bsz=bsz, seq=seq, head_dim=sb_dim)
            xs = _nmm(o, sb_w_out[j].astype(BF16), seq=seq, resid=(xs, mod, 2), name="sb_out")
        xs = _peer_ffn(xs, norm_ffn_g[layer].reshape(1, d), mod, peer_w_q[layer], peer_sub_keys[layer],
                       peer_u[layer], peer_v[layer], seq=seq)
    return xs.reshape(bsz, seq, d)
```

```python
import functools

import jax
import jax.numpy as jnp
import numpy as np
from jax import lax
from jax.experimental import pallas as pl
from jax.experimental.pallas import tpu as pltpu

F32 = jnp.float32
BF16 = jnp.bfloat16
I32 = jnp.int32

EPS = 1e-6
PEER_TOPK = 16
LANES = 128
SUBLANES = 8
VMEM_LIMIT_BYTES = 56 * 1024 * 1024

MLSTM_CHUNK = 256
SB_TILE_Q = 256
SB_TILE_K = 256
SB_LANES = 512
LOG2E = 1.4426950408889634
MM_TILE_M = 512
ROUTE_TILE = 512
ROUTE_SUB = 128
PEER_TILE = 64
PEER_GROUP = 8
IDX_SPLIT = 8
BITREV8 = (0, 4, 2, 6, 1, 5, 3, 7)


def _cparams(sem):
    return pltpu.CompilerParams(dimension_semantics=sem, vmem_limit_bytes=VMEM_LIMIT_BYTES)


def _dot(a, b):
    return jnp.dot(a, b, preferred_element_type=F32)


def _dot_tb(a, b):
    return lax.dot_general(a, b, (((1,), (1,)), ((), ())), preferred_element_type=F32)


def _split_dot(x, m):
    hi = x.astype(BF16)
    lo = (x - hi.astype(F32)).astype(BF16)
    return _dot(hi, m) + _dot(lo, m)


def _sigmoid(x):
    return 1.0 / (1.0 + jnp.exp(-x))


def _softplus(x):
    return jnp.maximum(x, 0.0) + jnp.log1p(jnp.exp(-jnp.abs(x)))


def _rmsnorm_mod(x, g, shift, scale):
    ms = jnp.mean(x * x, axis=-1, keepdims=True)
    y = x * lax.rsqrt(ms + EPS) * g
    return y * (1.0 + scale) + shift


def _ada_body(c_ref, w_ref, b_ref, o_ref):
    c = c_ref[...]
    a = (c * _sigmoid(c)).astype(BF16)
    o_ref[...] = _dot(a, w_ref[...].astype(BF16)) + b_ref[...]


def _ada(c, w, b):
    nl, d, e = w.shape
    bsz = c.shape[0]
    te = 1024
    return pl.pallas_call(
        _ada_body,
        out_shape=jax.ShapeDtypeStruct((nl, bsz, e), F32),
        grid=(nl, e // te),
        in_specs=[
            pl.BlockSpec((bsz, d), lambda l, j: (0, 0)),
            pl.BlockSpec((None, d, te), lambda l, j: (l, 0, j)),
            pl.BlockSpec((None, 1, te), lambda l, j: (l, 0, j)),
        ],
        out_specs=pl.BlockSpec((None, bsz, te), lambda l, j: (l, 0, j)),
        compiler_params=_cparams(("arbitrary", "arbitrary")),
        name="ada_mod",
    )(c, w, b.reshape(nl, 1, e))


def _nmm_body(*refs, prologue, epilogue, head_dim):
    it = iter(refs)
    x_ref = next(it)
    if prologue:
        g_ref, sh_ref, sc_ref = next(it), next(it), next(it)
    w_ref = next(it)
    if epilogue == "resid":
        res_ref, gate_ref = next(it), next(it)
    if epilogue == "headnorm":
        hg_ref = next(it)
    o_ref = next(it)
    h_ref = next(it)

    @pl.when(pl.program_id(1) == 0)
    def _():
        x = x_ref[...]
        if prologue:
            x = _rmsnorm_mod(x, g_ref[...], sh_ref[...], sc_ref[...])
        h_ref[...] = x.astype(BF16)

    acc = _dot(h_ref[...], w_ref[...])
    if epilogue == "resid":
        acc = res_ref[...] + gate_ref[...] * acc
    elif epilogue == "headnorm":
        r = lax.broadcasted_iota(I32, (LANES, LANES), 0) // head_dim
        c = lax.broadcasted_iota(I32, (LANES, LANES), 1) // head_dim
        group = jnp.where(r == c, 1.0, 0.0).astype(BF16)
        sq = acc * acc
        ms = jnp.concatenate(
            [_split_dot(sq[:, j * LANES:(j + 1) * LANES], group) for j in range(acc.shape[1] // LANES)],
            axis=1) * (1.0 / head_dim)
        acc = acc * lax.rsqrt(ms + EPS) * hg_ref[...]
    o_ref[...] = acc.astype(o_ref.dtype)


def _nmm(x, w, *, seq, norm=None, resid=None, headnorm=None, out_dtype=F32, tn=None, name):
    n, d = x.shape
    e = w.shape[1]
    tm = min(MM_TILE_M, seq)
    tn = e if tn is None else tn
    tpb = seq // tm
    args = [x]
    in_specs = [pl.BlockSpec((tm, d), lambda i, j: (i, 0))]
    if norm is not None:
        gain, mod, i_sh, i_sc = norm
        args += [gain, mod, mod]
        in_specs += [
            pl.BlockSpec((1, d), lambda i, j: (0, 0)),
            pl.BlockSpec((None, None, 1, d), lambda i, j: (i // tpb, i_sh, 0, 0)),
            pl.BlockSpec((None, None, 1, d), lambda i, j: (i // tpb, i_sc, 0, 0)),
        ]
    args.append(w)
    in_specs.append(pl.BlockSpec((d, tn), lambda i, j: (0, j)))
    epilogue = None
    head_dim = 0
    if resid is not None:
        res, mod, i_g = resid
        epilogue = "resid"
        args += [res, mod]
        in_specs += [
            pl.BlockSpec((tm, tn), lambda i, j: (i, j)),
            pl.BlockSpec((None, None, 1, tn), lambda i, j: (i // tpb, i_g, 0, j)),
        ]
    if headnorm is not None:
        hg, head_dim = headnorm
        epilogue = "headnorm"
        args.append(hg)
        in_specs.append(pl.BlockSpec((1, tn), lambda i, j: (0, j)))
    return pl.pallas_call(
        functools.partial(_nmm_body, prologue=norm is not None, epilogue=epilogue, head_dim=head_dim),
        out_shape=jax.ShapeDtypeStruct((n, e), out_dtype),
        grid=(n // tm, e // tn),
        in_specs=in_specs,
        out_specs=pl.BlockSpec((tm, tn), lambda i, j: (i, j)),
        scratch_shapes=[pltpu.VMEM((tm, d), BF16)],
        compiler_params=_cparams(("arbitrary", "arbitrary")),
        name=name,
    )(*args)


def _mlstm_body(qk_ref, v_ref, o_ref, gt_ref, cw_ref, bif_ref, hg_ref, out_ref,
                tail_ref, c_ref, m_ref, *, heads, qk_dim, conv_w):
    L = qk_ref.shape[0]
    nqk = heads * qk_dim
    vdim = LANES

    @pl.when(pl.program_id(1) == 0)
    def _():
        tail_ref[...] = jnp.zeros_like(tail_ref)
        c_ref[...] = jnp.zeros_like(c_ref)
        m_ref[...] = jnp.zeros_like(m_ref)

    x = qk_ref[...]
    tail = tail_ref[...]
    rows8 = lax.broadcasted_iota(I32, (SUBLANES, x.shape[1]), 0)
    acc = x * cw_ref[conv_w - 1:conv_w, :]
    for j in range(1, conv_w):
        xs = pltpu.roll(x, j, axis=0)
        head8 = jnp.where(rows8 < j, pltpu.roll(tail, j, axis=0), xs[0:SUBLANES])
        xs = jnp.concatenate([head8, xs[SUBLANES:]], axis=0)
        acc = acc + xs * cw_ref[conv_w - 1 - j:conv_w - j, :]
    tail_ref[...] = x[L - SUBLANES:, :]
    qk = acc * _sigmoid(acc)
    q_all = qk[:, :nqk] * (qk_dim ** -0.5)
    k_all = qk[:, nqk:]
    kt_all = k_all.T.astype(BF16)

    gates = gt_ref[...] + bif_ref[...]
    log_f = -_softplus(-gates)
    r = lax.broadcasted_iota(I32, (L, L), 0)
    c = lax.broadcasted_iota(I32, (L, L), 1)
    causal = r >= c
    tri = jnp.where(causal, 1.0, 0.0).astype(BF16)
    lf_hi = log_f.astype(BF16)
    lf_lo = (log_f - lf_hi.astype(F32)).astype(BF16)
    bcum = _dot(tri, lf_hi) + _dot(tri, lf_lo)
    gates_t = gates.T
    bcum_t = bcum.T
    lane = lax.broadcasted_iota(I32, (L, LANES), 1)
    ones_v = jnp.ones((L, vdim), BF16)
    rows_c = lax.broadcasted_iota(I32, (LANES, 2 * vdim), 0)

    for h in range(heads):
        pair, half = h // 2, h % 2
        b_col = bcum[:, heads + h:heads + h + 1]
        i_col = gates[:, h:h + 1]
        b_row = bcum_t[heads + h:heads + h + 1, :]
        i_row = gates_t[h:h + 1, :]
        b_last = b_col[L - 1:L, :]
        m_prev = m_ref[h:h + 1, 0:1]
        in_head = (lane >= half * qk_dim) & (lane < (half + 1) * qk_dim)
        q_h = jnp.where(in_head, q_all[:, pair * LANES:(pair + 1) * LANES], 0.0).astype(BF16)
        kt_h = kt_all[pair * LANES:(pair + 1) * LANES, :]
        v_h = v_ref[:, h * vdim:(h + 1) * vdim]
        v_aug = jnp.concatenate([v_h.astype(BF16), ones_v], axis=1)

        log_d = jnp.where(causal, b_col - b_row + i_row, -jnp.inf)
        inter_log = b_col + m_prev
        m_t = jnp.maximum(inter_log, jnp.max(log_d, axis=1, keepdims=True))
        inter_w = jnp.exp(inter_log - m_t)
        s = _dot(q_h, kt_h) * jnp.exp(log_d - m_t)
        c_prev = c_ref[h]
        tot = _dot(s.astype(BF16), v_aug) + inter_w * _dot(q_h, c_prev.astype(BF16))
        num = tot[:, :vdim]
        den = tot[:, vdim:]
        hh = num / jnp.maximum(jnp.abs(den), jnp.exp(-m_t))
        ms = jnp.mean(hh * hh, axis=1, keepdims=True)
        hh = hh * lax.rsqrt(ms + EPS) * hg_ref[:, h * vdim:(h + 1) * vdim]
        hh = hh * _sigmoid(o_ref[:, h * vdim:(h + 1) * vdim])
        out_ref[:, h * vdim:(h + 1) * vdim] = hh.astype(out_ref.dtype)

        w_state = b_last - b_col + i_col
        m_loc = jnp.max(w_state, axis=0, keepdims=True)
        e_state = jnp.exp(w_state - m_loc)
        ev = (e_state * jnp.concatenate([v_h, jnp.ones((L, vdim), F32)], axis=1)).astype(BF16)
        c_loc = _dot(kt_h, ev)
        in_rows = (rows_c >= half * qk_dim) & (rows_c < (half + 1) * qk_dim)
        c_loc = jnp.where(in_rows, c_loc, 0.0)
        m_new = jnp.maximum(b_last + m_prev, m_loc)
        a = jnp.exp(b_last + m_prev - m_new)
        rr = jnp.exp(m_loc - m_new)
        c_ref[h] = a * c_prev + rr * c_loc
        m_ref[h:h + 1, :] = jnp.broadcast_to(m_new, (1, LANES))


def _mlstm(proj, conv_w, b_if, hnorm_g, *, bsz, seq, d):
    heads, vdim = hnorm_g.shape
    assert vdim == LANES
    nqk2 = conv_w.shape[1]
    qk_dim = nqk2 // (2 * heads)
    width = conv_w.shape[0]
    L = min(MLSTM_CHUNK, seq)
    nc = seq // L
    bif = jnp.zeros((1, LANES), F32).at[0, :2 * heads].set(b_if)
    gate_blk = (nqk2 + 2 * d) // LANES
    return pl.pallas_call(
        functools.partial(_mlstm_body, heads=heads, qk_dim=qk_dim, conv_w=width),
        out_shape=jax.ShapeDtypeStruct((bsz * seq, d), BF16),
        grid=(bsz, nc),
        in_specs=[
            pl.BlockSpec((L, nqk2), lambda b, c: (b * nc + c, 0)),
            pl.BlockSpec((L, d), lambda b, c: (b * nc + c, nqk2 // d)),
            pl.BlockSpec((L, d), lambda b, c: (b * nc + c, nqk2 // d + 1)),
            pl.BlockSpec((L, LANES), lambda b, c: (b * nc + c, gate_blk)),
            pl.BlockSpec((width, nqk2), lambda b, c: (0, 0)),
            pl.BlockSpec((1, LANES), lambda b, c: (0, 0)),
            pl.BlockSpec((1, d), lambda b, c: (0, 0)),
        ],
        out_specs=pl.BlockSpec((L, d), lambda b, c: (b * nc + c, 0)),
        scratch_shapes=[
            pltpu.VMEM((SUBLANES, nqk2), F32),
            pltpu.VMEM((heads, LANES, 2 * LANES), F32),
            pltpu.VMEM((heads, LANES), F32),
        ],
        compiler_params=_cparams(("arbitrary", "arbitrary")),
        name="mlstm",
    )(proj, proj, proj, proj, conv_w, bif, hnorm_g.reshape(1, d))


def _sb_body(q_ref, k_ref, v_ref, o_ref, *, head_dim):
    TQ = q_ref.shape[0]
    TK = SB_TILE_K
    kpq = TQ // TK
    npair = q_ref.shape[1] // LANES
    nh = 2 * npair
    qi = pl.program_id(2)
    lane = lax.broadcasted_iota(I32, (TQ, LANES), 1)
    first = lane < head_dim
    q_heads = []
    for p in range(npair):
        q = q_ref[:, p * LANES:(p + 1) * LANES].astype(F32)
        q_heads += [jnp.where(first, q, 0.0).astype(BF16), jnp.where(first, 0.0, q).astype(BF16)]
    r = lax.broadcasted_iota(I32, (TK, TK), 0)
    c = lax.broadcasted_iota(I32, (TK, TK), 1)
    later = jnp.where(r >= c, 1.0, 0.0).astype(BF16)
    row = lax.broadcasted_iota(I32, (TQ, TK), 0)
    col = lax.broadcasted_iota(I32, (TQ, TK), 1)

    def block(kb, carry, dk):
        off = pl.multiple_of(kb * TK, TK)
        strict = None if dk is None else (col + dk * TK) < row
        kblk = [k_ref[pl.ds(off, TK), p * LANES:(p + 1) * LANES] for p in range(npair)]
        vblk = [v_ref[pl.ds(off, TK), p * LANES:(p + 1) * LANES] for p in range(npair)]
        zs = [_dot_tb(q_heads[h], kblk[h // 2]) for h in range(nh)]
        ls = []
        for z in zs:
            l = jnp.maximum(z, 0.0) + jnp.log2(1.0 + jnp.exp2(-jnp.abs(z)))
            ls.append(l if strict is None else jnp.where(strict, l, 0.0))
        tots = [_split_dot(ls[h], later) + carry[2 * h + 1] for h in range(nh)]
        new = []
        for h in range(nh):
            a = jnp.exp2(zs[h] - tots[h])
            if strict is not None:
                a = jnp.where(strict, a, 0.0)
            new += [carry[2 * h] + _dot(a.astype(BF16), vblk[h // 2]),
                    jnp.broadcast_to(tots[h][:, 0:1], (TQ, TK))]
        return tuple(new)

    carry = []
    for _ in range(nh):
        carry += [jnp.zeros((TQ, LANES), F32), jnp.zeros((TQ, TK), F32)]
    carry = tuple(carry)
    for dk in reversed(range(kpq)):
        carry = block(qi * kpq + dk, carry, dk)
    carry = lax.fori_loop(0, qi * kpq, lambda j, cr: block(qi * kpq - 1 - j, cr, None), carry)
    for p in range(npair):
        o_ref[:, p * LANES:(p + 1) * LANES] = jnp.where(first, carry[4 * p], carry[4 * p + 2]).astype(o_ref.dtype)


def _sb_attention(q, k, v, *, bsz, seq, head_dim):
    n, d = q.shape
    assert 2 * head_dim == LANES
    t = min(SB_TILE_Q, seq)
    assert t % SB_TILE_K == 0
    w = min(SB_LANES, d)
    q3, k3, v3 = (a.reshape(bsz, seq, d) for a in (q, k, v))
    out = pl.pallas_call(
        functools.partial(_sb_body, head_dim=head_dim),
        out_shape=jax.ShapeDtypeStruct((bsz, seq, d), BF16),
        grid=(bsz, d // w, seq // t),
        in_specs=[
            pl.BlockSpec((None, t, w), lambda b, hp, i: (b, i, hp)),
            pl.BlockSpec((None, seq, w), lambda b, hp, i: (b, 0, hp)),
            pl.BlockSpec((None, seq, w), lambda b, hp, i: (b, 0, hp)),
        ],
        out_specs=pl.BlockSpec((None, t, w), lambda b, hp, i: (b, i, hp)),
        compiler_params=_cparams(("arbitrary", "arbitrary", "arbitrary")),
        name="sb_attention",
    )(q3, k3, v3)
    return out.reshape(n, d)


def _topk_rows(problems, k, rows=None, big=None):
    problems = list(problems)
    if rows is None:
        rows = lax.broadcasted_iota(I32, problems[0].shape, 0).astype(F32)
        big = float(problems[0].shape[0])
    vals = [[] for _ in problems]
    idxs = [[] for _ in problems]
    for _ in range(k):
        for j, s in enumerate(problems):
            m = jnp.max(s, axis=0, keepdims=True)
            i = jnp.min(jnp.where(s == m, rows, big), axis=0, keepdims=True)
            vals[j].append(m)
            idxs[j].append(i)
            problems[j] = jnp.where(rows == i, -jnp.inf, s)
    return [(jnp.concatenate(v, axis=0), jnp.concatenate(i, axis=0)) for v, i in zip(vals, idxs)]


def _select_rows(table, sel, k):
    out = jnp.zeros_like(sel)
    for a in range(k):
        out = jnp.where(sel == float(a), table[a:a + 1, :], out)
    return out


def _route_body(x_ref, g_ref, sh_ref, sc_ref, wq_ref, keys_ref, h_ref, idx_ref, gate_ref,
                q_scr, idx_t, gate_t, *, heads, nkeys):
    k = PEER_TOPK
    tm = x_ref.shape[0]
    hmod = _rmsnorm_mod(x_ref[...], g_ref[...], sh_ref[...], sc_ref[...])
    h_ref[...] = hmod
    q_scr[...] = _dot(hmod.astype(BF16), wq_ref[...]).astype(BF16)
    nsub = tm // ROUTE_SUB
    ncand = sum(k // (a + 1) for a in range(k))
    cand_pad = (-ncand) % SUBLANES
    ridx = lax.broadcasted_iota(I32, (ncand + cand_pad, ROUTE_SUB), 0)
    cand_pos = jnp.full(ridx.shape, k * k, I32)
    start = 0
    for a in range(k):
        nb = k // (a + 1)
        cand_pos = jnp.where((ridx >= start) & (ridx < start + nb), ridx + (a * k - start), cand_pos)
        start += nb
    cand_pos = cand_pos.astype(F32)

    def head_step(hd, carry):
        scores = []
        for sub in range(nsub):
            for p in range(2):
                c0 = pl.multiple_of((hd * 2 + p) * LANES, LANES)
                qs = q_scr[sub * ROUTE_SUB:(sub + 1) * ROUTE_SUB, pl.ds(c0, LANES)]
                scores.append(_dot_tb(keys_ref[p], qs))
        tops = _topk_rows(scores, k)
        cands = []
        for sub in range(nsub):
            (v0, _), (v1, _) = tops[2 * sub], tops[2 * sub + 1]
            pieces = [v0[a:a + 1, :] + v1[0:k // (a + 1), :] for a in range(k)]
            pieces.append(jnp.full((cand_pad, ROUTE_SUB), -jnp.inf, F32))
            cands.append(jnp.concatenate(pieces, axis=0))
        best = _topk_rows(cands, k, rows=cand_pos, big=float(k * k))
        r0 = pl.multiple_of(hd * k, k)
        for sub in range(nsub):
            (_, i0), (_, i1) = tops[2 * sub], tops[2 * sub + 1]
            g_top, pos = best[sub]
            pa = jnp.floor(pos * (1.0 / k))
            pb = pos - pa * k
            expert = (_select_rows(i0, pa, k) * float(nkeys) + _select_rows(i1, pb, k)) * float(SUBLANES // 2)
            e = jnp.exp(g_top - g_top[0:1, :])
            gate = e / jnp.sum(e, axis=0, keepdims=True)
            idx_t[pl.ds(r0, k), sub * ROUTE_SUB:(sub + 1) * ROUTE_SUB] = expert
            gate_t[pl.ds(r0, k), sub * ROUTE_SUB:(sub + 1) * ROUTE_SUB] = gate
        return carry

    lax.fori_loop(0, heads, head_step, 0)
    idx_ref[...] = idx_t[...].T.astype(I32)
    gate_ref[...] = gate_t[...].T


def _peer_route(x, gain, mod, i_sh, i_sc, wq, keys, *, seq):
    n, d = x.shape
    qcols = wq.shape[1]
    nkeys, kdim = keys.shape[1], keys.shape[2]
    assert kdim == LANES and nkeys == LANES
    heads = qcols // (2 * kdim)
    tm = min(ROUTE_TILE, seq)
    tpb = seq // tm
    ha = heads * PEER_TOPK
    return pl.pallas_call(
        functools.partial(_route_body, heads=heads, nkeys=nkeys),
        out_shape=(jax.ShapeDtypeStruct((n, d), F32),
                   jax.ShapeDtypeStruct((n, ha), I32),
                   jax.ShapeDtypeStruct((n, ha), F32)),
        grid=(n // tm,),
        in_specs=[
            pl.BlockSpec((tm, d), lambda i: (i, 0)),
            pl.BlockSpec((1, d), lambda i: (0, 0)),
            pl.BlockSpec((None, None, 1, d), lambda i: (i // tpb, i_sh, 0, 0)),
            pl.BlockSpec((None, None, 1, d), lambda i: (i // tpb, i_sc, 0, 0)),
            pl.BlockSpec((d, qcols), lambda i: (0, 0)),
            pl.BlockSpec((2, nkeys, kdim), lambda i: (0, 0, 0)),
        ],
        out_specs=(pl.BlockSpec((tm, d), lambda i: (i, 0)),
                   pl.BlockSpec((tm, ha), lambda i: (i, 0)),
                   pl.BlockSpec((tm, ha), lambda i: (i, 0))),
        scratch_shapes=[pltpu.VMEM((tm, qcols), BF16),
                        pltpu.VMEM((ha, tm), F32),
                        pltpu.VMEM((ha, tm), F32)],
        compiler_params=_cparams(("arbitrary",)),
        name="peer_route",
    )(x, gain, mod, mod, wq, keys)


def _pack_table(t):
    e, d = t.shape
    assert d == SUBLANES * LANES
    half = SUBLANES // 2
    bits = lax.bitcast_convert_type(t.astype(BF16).reshape(e, 2, half, LANES), jnp.uint16).astype(jnp.uint32)
    words = lax.bitcast_convert_type(bits[:, 0] | (bits[:, 1] << 16), I32).reshape(e * half, LANES)
    return jnp.pad(words, ((half, half), (0, 0)))


def _row_of(idx_refs, t, a):
    return idx_refs[a % IDX_SPLIT][t, a // IDX_SPLIT]


def _window_pair(tbl_ref, row_a, row_b, low_half):
    half = SUBLANES // 2
    wa = tbl_ref[pl.ds(row_a + half, SUBLANES), :]
    wb = tbl_ref[pl.ds(row_b, SUBLANES), :]
    return jnp.where(low_half, wa, wb)


def _unpack(w):
    lo = lax.bitcast_convert_type(w << 16, F32)
    hi = lax.bitcast_convert_type(w & jnp.int32(-65536), F32)
    return lo, hi


def _merge(v1, v2, g, first_half):
    h = g // 2
    a = v1 + pltpu.roll(v1, SUBLANES - h, axis=0)
    b = v2 + pltpu.roll(v2, h, axis=0)
    return jnp.where(first_half, a, b)


def _erf(x):
    ax = jnp.abs(x)
    t = 1.0 / (1.0 + 0.3275911 * ax)
    poly = ((((1.061405429 * t - 1.453152027) * t + 1.421413741) * t - 0.284496736) * t + 0.254829592) * t
    y = 1.0 - poly * jnp.exp(-ax * ax)
    return jnp.where(x < 0, -y, y)


def _peer_u_body(*refs):
    idx_refs = refs[:IDX_SPLIT]
    tbl_ref, x_ref, g_ref, w_ref = refs[IDX_SPLIT:IDX_SPLIT + 4]
    parts = refs[IDX_SPLIT + 4:]
    tt, na = g_ref.shape
    half = SUBLANES // 2
    sub = lax.broadcasted_iota(I32, (SUBLANES, LANES), 0)
    low_half = sub < half
    first4 = (sub & 3) < 2
    first2 = (sub & 1) < 1
    sub4 = [(sub & 3) == k for k in range(3)]

    def finish(t, part_ref):
        act = jnp.sum(part_ref[...].T, axis=0, keepdims=True)
        gelu = 0.5 * act * (1.0 + _erf(act * (2.0 ** -0.5)))
        return g_ref[pl.ds(t, 1), :] * gelu

    def gather(t, part_ref):
        xrow = x_ref[pl.ds(t, 1), :]
        xc = [jnp.broadcast_to(xrow[:, c * LANES:(c + 1) * LANES], (SUBLANES, LANES)) for c in range(SUBLANES)]
        xl = jnp.where(sub4[0], xc[0], jnp.where(sub4[1], xc[1], jnp.where(sub4[2], xc[2], xc[3])))
        xh = jnp.where(sub4[0], xc[4], jnp.where(sub4[1], xc[5], jnp.where(sub4[2], xc[6], xc[7])))
        groups = []
        for grp in range(na // SUBLANES):
            c = []
            for i in range(half):
                a0 = grp * SUBLANES + BITREV8[2 * i]
                a1 = grp * SUBLANES + BITREV8[2 * i + 1]
                lo, hi = _unpack(_window_pair(tbl_ref, _row_of(idx_refs, t, a0), _row_of(idx_refs, t, a1),
                                              low_half))
                c.append(lo * xl + hi * xh)
            d0 = _merge(c[0], c[1], 4, first4)
            d1 = _merge(c[2], c[3], 4, first4)
            groups.append(_merge(d0, d1, 2, first2))
        part_ref[...] = jnp.concatenate(groups, axis=0)

    grp = len(parts)

    def token_group(i, carry):
        t = grp * i
        done = [finish(t - grp + k, parts[k]) for k in range(grp)]
        for k in range(grp):
            gather(t + k, parts[k])
        for k in range(grp):
            w_ref[pl.ds(t - grp + k, 1), :] = done[k]
        return carry

    for k in range(grp):
        gather(k, parts[k])
    lax.fori_loop(1, tt // grp, token_group, 0)
    for k in range(grp):
        w_ref[pl.ds(tt - grp + k, 1), :] = finish(tt - grp + k, parts[k])


def _peer_v_body(*refs):
    idx_refs = refs[:IDX_SPLIT]
    tbl_ref, rep_ref, keep_ref, w_ref, res_ref, gate_ref, o_ref, spread_ref = refs[IDX_SPLIT:IDX_SPLIT + 8]
    rows = refs[IDX_SPLIT + 8:]
    tt = res_ref.shape[0]
    na = w_ref.shape[1]
    half = SUBLANES // 2
    low_half = lax.broadcasted_iota(I32, (SUBLANES, LANES), 0) < half

    w_all = w_ref[...]
    w_hi = w_all.astype(BF16)
    w_lo = (w_all - w_hi.astype(F32)).astype(BF16)
    spread_ref[...] = _dot(jnp.concatenate([w_hi, w_lo], axis=0), rep_ref[...])
    keep = keep_ref[...]

    def gather(t, rows_ref):
        for j in range(na // 2):
            tile = _window_pair(tbl_ref, _row_of(idx_refs, t, 2 * j), _row_of(idx_refs, t, 2 * j + 1), low_half)
            rows_ref[j * SUBLANES:(j + 1) * SUBLANES, :] = tile

    def finish(t, rows_ref):
        lhs = jnp.concatenate([spread_ref[pl.ds(t, 1), :] * keep, spread_ref[pl.ds(tt + t, 1), :] * keep],
                              axis=0).astype(BF16)
        out = _dot(lhs, pltpu.bitcast(rows_ref[...], BF16))
        y = out[0:SUBLANES] + out[SUBLANES:]
        yrow = jnp.concatenate([y[c:c + 1, :] for c in range(SUBLANES)], axis=1)
        return res_ref[pl.ds(t, 1), :] + gate_ref[...] * yrow

    def store(t, row):
        o_ref[pl.ds(t, 1), :] = row

    grp = len(rows)

    def token_group(i, carry):
        t = grp * i
        done = [finish(t - grp + k, rows[k]) for k in range(grp)]
        for k in range(grp):
            gather(t + k, rows[k])
        for k in range(grp):
            store(t - grp + k, done[k])
        return carry

    for k in range(grp):
        gather(k, rows[k])
    lax.fori_loop(1, tt // grp, token_group, 0)
    for k in range(grp):
        store(tt - grp + k, finish(tt - grp + k, rows[k]))


def _v_side_constants(na):
    kk = np.arange(na * SUBLANES)
    rep = (np.arange(na)[:, None] == (2 * (kk // 16) + (kk % 16) // 8)[None, :]).astype(np.float32)
    keep = (np.arange(SUBLANES)[:, None] == (((kk % 16) // 2) % 4 + 4 * (kk % 2))[None, :]).astype(np.float32)
    return jnp.asarray(rep, BF16), jnp.asarray(keep, F32)


def _peer_experts(x, hmod, idx, gate, tbl_u, tbl_v, mod, i_gate, *, seq):
    n, d = x.shape
    na = idx.shape[1]
    tt = min(PEER_TILE, seq)
    tpb = seq // tt
    tbl_spec = pl.BlockSpec(memory_space=pltpu.VMEM)
    idx_split = idx.reshape(n, na // IDX_SPLIT, IDX_SPLIT).transpose(2, 0, 1)
    idx_specs = [pl.BlockSpec((None, tt, na // IDX_SPLIT), functools.partial(lambda k, i: (k, i, 0), k),
                              memory_space=pltpu.SMEM) for k in range(IDX_SPLIT)]
    idx_args = [idx_split] * IDX_SPLIT
    w = pl.pallas_call(
        _peer_u_body,
        out_shape=jax.ShapeDtypeStruct((n, na), F32),
        grid=(n // tt,),
        in_specs=idx_specs + [
            tbl_spec,
            pl.BlockSpec((tt, d), lambda i: (i, 0)),
            pl.BlockSpec((tt, na), lambda i: (i, 0)),
        ],
        out_specs=pl.BlockSpec((tt, na), lambda i: (i, 0)),
        scratch_shapes=[pltpu.VMEM((na, LANES), F32)] * PEER_GROUP,
        compiler_params=_cparams(("arbitrary",)),
        name="peer_u",
    )(*idx_args, tbl_u, hmod, gate)
    rep, keep = _v_side_constants(na)
    whole = lambda a: pl.BlockSpec(a.shape, lambda i: (0, 0))
    out = pl.pallas_call(
        _peer_v_body,
        out_shape=jax.ShapeDtypeStruct((n, d), F32),
        grid=(n // tt,),
        in_specs=idx_specs + [
            tbl_spec,
            whole(rep), whole(keep),
            pl.BlockSpec((tt, na), lambda i: (i, 0)),
            pl.BlockSpec((tt, d), lambda i: (i, 0)),
            pl.BlockSpec((None, None, 1, d), lambda i: (i // tpb, i_gate, 0, 0)),
        ],
        out_specs=pl.BlockSpec((tt, d), lambda i: (i, 0)),
        scratch_shapes=[pltpu.VMEM((2 * tt, na * SUBLANES), F32)]
        + [pltpu.VMEM((na * SUBLANES // 2, LANES), I32)] * PEER_GROUP,
        compiler_params=_cparams(("arbitrary",)),
        name="peer_v",
    )(*idx_args, tbl_v, rep, keep, w, x, mod)
    return out


def _peer_ffn(x, gain, mod, wq, sub_keys, u, v, *, seq):
    hmod, idx, gate = _peer_route(x, gain, mod, 3, 4, wq.astype(BF16), sub_keys.astype(BF16), seq=seq)
    return _peer_experts(x, hmod, idx, gate, _pack_table(u), _pack_table(v), mod, 5, seq=seq)


def kernel(x, c, ada_w, ada_b, norm_mix_g, norm_ffn_g, ma_w_in, ma_conv_w, ma_b_if, ma_hnorm_g, ma_w_out,
           kv_ada_w, kv_ada_b, kv_norm_g, kv_w, k_norm_g, sb_w_q, sb_q_norm_g, sb_w_out,
           peer_w_q, peer_sub_keys, peer_u, peer_v):
    bsz, seq, d = x.shape
    n = bsz * seq
    depth = ada_w.shape[0]
    n_a = ma_w_in.shape[0]
    sb_dim = k_norm_g.shape[0]
    sb_heads = d // sb_dim

    mods = _ada(c, ada_w, ada_b).reshape(depth, bsz, 6, 1, d)
    kv_mod = _ada(c, kv_ada_w[None], kv_ada_b[None]).reshape(bsz, 2, 1, d)
    xs = x.reshape(n, d)
    k_sh = v_sh = None
    for layer in range(depth):
        mod = mods[layer]
        g_mix = norm_mix_g[layer].reshape(1, d)
        if layer < n_a:
            w_in = ma_w_in[layer]
            pad = (-w_in.shape[1]) % LANES
            w_in = jnp.pad(w_in, ((0, 0), (0, pad))).astype(BF16)
            proj = _nmm(xs, w_in, seq=seq, norm=(g_mix, mod, 0, 1), name="mlstm_proj")
            hh = _mlstm(proj, ma_conv_w[layer], ma_b_if[layer], ma_hnorm_g[layer], bsz=bsz, seq=seq, d=d)
            xs = _nmm(hh, ma_w_out[layer].astype(BF16), seq=seq, resid=(xs, mod, 2), name="mlstm_out")
        else:
            if layer == n_a:
                kvw = kv_w.astype(BF16)
                k_gain = jnp.tile(k_norm_g, sb_heads).reshape(1, d)
                g_kv = kv_norm_g.reshape(1, d)
                k_sh = _nmm(xs, kvw[:, :d], seq=seq, norm=(g_kv, kv_mod, 0, 1),
                            headnorm=(k_gain, sb_dim), out_dtype=BF16, name="kv_k")
                v_sh = _nmm(xs, kvw[:, d:], seq=seq, norm=(g_kv, kv_mod, 0, 1), out_dtype=BF16, name="kv_v")
            j = layer - n_a
            q_gain = (jnp.tile(sb_q_norm_g[j], sb_heads) * (sb_dim ** -0.5 * LOG2E)).reshape(1, d)
            q = _nmm(xs, sb_w_q[j].astype(BF16), seq=seq, norm=(g_mix, mod, 0, 1),
                     headnorm=(q_gain, sb_dim), out_dtype=BF16, name="sb_q")
            o = _sb_attention(q, k_sh, v_sh, bsz=bsz, seq=seq, head_dim=sb_dim)
            xs = _nmm(o, sb_w_out[j].astype(BF16), seq=seq, resid=(xs, mod, 2), name="sb_out")
        xs = _peer_ffn(xs, norm_ffn_g[layer].reshape(1, d), mod, peer_w_q[layer], peer_sub_keys[layer],
                       peer_u[layer], peer_v[layer], seq=seq)
    return xs.reshape(bsz, seq, d)
```

```python
import functools

import jax
import jax.numpy as jnp
import numpy as np
from jax import lax
from jax.experimental import pallas as pl
from jax.experimental.pallas import tpu as pltpu

F32 = jnp.float32
BF16 = jnp.bfloat16
I32 = jnp.int32

EPS = 1e-6
PEER_TOPK = 16
LANES = 128
SUBLANES = 8
VMEM_LIMIT_BYTES = 56 * 1024 * 1024

MLSTM_CHUNK = 256
SB_TILE_Q = 256
SB_TILE_K = 256
SB_LANES = 512
LOG2E = 1.4426950408889634
MM_TILE_M = 512
ROUTE_TILE = 1024
ROUTE_SUB = 128
PEER_TILE = 256
PEER_GROUP = 16
IDX_SPLIT = 8
BITREV8 = (0, 4, 2, 6, 1, 5, 3, 7)


def _cparams(sem):
    return pltpu.CompilerParams(dimension_semantics=sem, vmem_limit_bytes=VMEM_LIMIT_BYTES)


def _dot(a, b):
    return jnp.dot(a, b, preferred_element_type=F32)


def _dot_tb(a, b):
    return lax.dot_general(a, b, (((1,), (1,)), ((), ())), preferred_element_type=F32)


def _split_dot(x, m):
    hi = x.astype(BF16)
    lo = (x - hi.astype(F32)).astype(BF16)
    return _dot(hi, m) + _dot(lo, m)


def _sigmoid(x):
    return 1.0 / (1.0 + jnp.exp(-x))


def _softplus(x):
    return jnp.maximum(x, 0.0) + jnp.log1p(jnp.exp(-jnp.abs(x)))


def _rmsnorm_mod(x, g, shift, scale):
    ms = jnp.mean(x * x, axis=-1, keepdims=True)
    y = x * lax.rsqrt(ms + EPS) * g
    return y * (1.0 + scale) + shift


def _ada_body(c_ref, w_ref, b_ref, o_ref):
    c = c_ref[...]
    a = (c * _sigmoid(c)).astype(BF16)
    o_ref[...] = _dot(a, w_ref[...].astype(BF16)) + b_ref[...]


def _ada(c, w, b):
    nl, d, e = w.shape
    bsz = c.shape[0]
    te = 1024
    return pl.pallas_call(
        _ada_body,
        out_shape=jax.ShapeDtypeStruct((nl, bsz, e), F32),
        grid=(nl, e // te),
        in_specs=[
            pl.BlockSpec((bsz, d), lambda l, j: (0, 0)),
            pl.BlockSpec((None, d, te), lambda l, j: (l, 0, j)),
            pl.BlockSpec((None, 1, te), lambda l, j: (l, 0, j)),
        ],
        out_specs=pl.BlockSpec((None, bsz, te), lambda l, j: (l, 0, j)),
        compiler_params=_cparams(("arbitrary", "arbitrary")),
        name="ada_mod",
    )(c, w, b.reshape(nl, 1, e))


def _nmm_body(*refs, prologue, epilogue, head_dim):
    it = iter(refs)
    x_ref = next(it)
    if prologue:
        g_ref, sh_ref, sc_ref = next(it), next(it), next(it)
    w_ref = next(it)
    if epilogue == "resid":
        res_ref, gate_ref = next(it), next(it)
    if epilogue == "headnorm":
        hg_ref = next(it)
    o_ref = next(it)
    h_ref = next(it)

    @pl.when(pl.program_id(1) == 0)
    def _():
        x = x_ref[...]
        if prologue:
            x = _rmsnorm_mod(x, g_ref[...], sh_ref[...], sc_ref[...])
        h_ref[...] = x.astype(BF16)

    acc = _dot(h_ref[...], w_ref[...])
    if epilogue == "resid":
        acc = res_ref[...] + gate_ref[...] * acc
    elif epilogue == "headnorm":
        r = lax.broadcasted_iota(I32, (LANES, LANES), 0) // head_dim
        c = lax.broadcasted_iota(I32, (LANES, LANES), 1) // head_dim
        group = jnp.where(r == c, 1.0, 0.0).astype(BF16)
        sq = acc * acc
        ms = jnp.concatenate(
            [_split_dot(sq[:, j * LANES:(j + 1) * LANES], group) for j in range(acc.shape[1] // LANES)],
            axis=1) * (1.0 / head_dim)
        acc = acc * lax.rsqrt(ms + EPS) * hg_ref[...]
    o_ref[...] = acc.astype(o_ref.dtype)


def _nmm(x, w, *, seq, norm=None, resid=None, headnorm=None, out_dtype=F32, tn=None, name):
    n, d = x.shape
    e = w.shape[1]
    tm = min(MM_TILE_M, seq)
    tn = e if tn is None else tn
    tpb = seq // tm
    args = [x]
    in_specs = [pl.BlockSpec((tm, d), lambda i, j: (i, 0))]
    if norm is not None:
        gain, mod, i_sh, i_sc = norm
        args += [gain, mod, mod]
        in_specs += [
            pl.BlockSpec((1, d), lambda i, j: (0, 0)),
            pl.BlockSpec((None, None, 1, d), lambda i, j: (i // tpb, i_sh, 0, 0)),
            pl.BlockSpec((None, None, 1, d), lambda i, j: (i // tpb, i_sc, 0, 0)),
        ]
    args.append(w)
    in_specs.append(pl.BlockSpec((d, tn), lambda i, j: (0, j)))
    epilogue = None
    head_dim = 0
    if resid is not None:
        res, mod, i_g = resid
        epilogue = "resid"
        args += [res, mod]
        in_specs += [
            pl.BlockSpec((tm, tn), lambda i, j: (i, j)),
            pl.BlockSpec((None, None, 1, tn), lambda i, j: (i // tpb, i_g, 0, j)),
        ]
    if headnorm is not None:
        hg, head_dim = headnorm
        epilogue = "headnorm"
        args.append(hg)
        in_specs.append(pl.BlockSpec((1, tn), lambda i, j: (0, j)))
    return pl.pallas_call(
        functools.partial(_nmm_body, prologue=norm is not None, epilogue=epilogue, head_dim=head_dim),
        out_shape=jax.ShapeDtypeStruct((n, e), out_dtype),
        grid=(n // tm, e // tn),
        in_specs=in_specs,
        out_specs=pl.BlockSpec((tm, tn), lambda i, j: (i, j)),
        scratch_shapes=[pltpu.VMEM((tm, d), BF16)],
        compiler_params=_cparams(("arbitrary", "arbitrary")),
        name=name,
    )(*args)


def _mlstm_body(qk_ref, v_ref, o_ref, gt_ref, cw_ref, bif_ref, hg_ref, out_ref,
                tail_ref, c_ref, m_ref, *, heads, qk_dim, conv_w):
    L = qk_ref.shape[0]
    nqk = heads * qk_dim
    vdim = LANES

    @pl.when(pl.program_id(1) == 0)
    def _():
        tail_ref[...] = jnp.zeros_like(tail_ref)
        c_ref[...] = jnp.zeros_like(c_ref)
        m_ref[...] = jnp.zeros_like(m_ref)

    x = qk_ref[...]
    tail = tail_ref[...]
    rows8 = lax.broadcasted_iota(I32, (SUBLANES, x.shape[1]), 0)
    acc = x * cw_ref[conv_w - 1:conv_w, :]
    for j in range(1, conv_w):
        xs = pltpu.roll(x, j, axis=0)
        head8 = jnp.where(rows8 < j, pltpu.roll(tail, j, axis=0), xs[0:SUBLANES])
        xs = jnp.concatenate([head8, xs[SUBLANES:]], axis=0)
        acc = acc + xs * cw_ref[conv_w - 1 - j:conv_w - j, :]
    tail_ref[...] = x[L - SUBLANES:, :]
    qk = acc * _sigmoid(acc)
    q_all = qk[:, :nqk] * (qk_dim ** -0.5)
    k_all = qk[:, nqk:]
    kt_all = k_all.T.astype(BF16)

    gates = gt_ref[...] + bif_ref[...]
    log_f = -_softplus(-gates)
    r = lax.broadcasted_iota(I32, (L, L), 0)
    c = lax.broadcasted_iota(I32, (L, L), 1)
    causal = r >= c
    tri = jnp.where(causal, 1.0, 0.0).astype(BF16)
    lf_hi = log_f.astype(BF16)
    lf_lo = (log_f - lf_hi.astype(F32)).astype(BF16)
    bcum = _dot(tri, lf_hi) + _dot(tri, lf_lo)
    gates_t = gates.T
    bcum_t = bcum.T
    lane = lax.broadcasted_iota(I32, (L, LANES), 1)
    ones_v = jnp.ones((L, vdim), BF16)
    rows_c = lax.broadcasted_iota(I32, (LANES, 2 * vdim), 0)

    for h in range(heads):
        pair, half = h // 2, h % 2
        b_col = bcum[:, heads + h:heads + h + 1]
        i_col = gates[:, h:h + 1]
        b_row = bcum_t[heads + h:heads + h + 1, :]
        i_row = gates_t[h:h + 1, :]
        b_last = b_col[L - 1:L, :]
        m_prev = m_ref[h:h + 1, 0:1]
        in_head = (lane >= half * qk_dim) & (lane < (half + 1) * qk_dim)
        q_h = jnp.where(in_head, q_all[:, pair * LANES:(pair + 1) * LANES], 0.0).astype(BF16)
        kt_h = kt_all[pair * LANES:(pair + 1) * LANES, :]
        v_h = v_ref[:, h * vdim:(h + 1) * vdim]
        v_aug = jnp.concatenate([v_h.astype(BF16), ones_v], axis=1)

        log_d = jnp.where(causal, b_col - b_row + i_row, -jnp.inf)
        inter_log = b_col + m_prev
        m_t = jnp.maximum(inter_log, jnp.max(log_d, axis=1, keepdims=True))
        inter_w = jnp.exp(inter_log - m_t)
        s = _dot(q_h, kt_h) * jnp.exp(log_d - m_t)
        c_prev = c_ref[h]
        tot = _dot(s.astype(BF16), v_aug) + inter_w * _dot(q_h, c_prev.astype(BF16))
        num = tot[:, :vdim]
        den = tot[:, vdim:]
        hh = num / jnp.maximum(jnp.abs(den), jnp.exp(-m_t))
        ms = jnp.mean(hh * hh, axis=1, keepdims=True)
        hh = hh * lax.rsqrt(ms + EPS) * hg_ref[:, h * vdim:(h + 1) * vdim]
        hh = hh * _sigmoid(o_ref[:, h * vdim:(h + 1) * vdim])
        out_ref[:, h * vdim:(h + 1) * vdim] = hh.astype(out_ref.dtype)

        w_state = b_last - b_col + i_col
        m_loc = jnp.max(w_state, axis=0, keepdims=True)
        e_state = jnp.exp(w_state - m_loc)
        ev = (e_state * jnp.concatenate([v_h, jnp.ones((L, vdim), F32)], axis=1)).astype(BF16)
        c_loc = _dot(kt_h, ev)
        in_rows = (rows_c >= half * qk_dim) & (rows_c < (half + 1) * qk_dim)
        c_loc = jnp.where(in_rows, c_loc, 0.0)
        m_new = jnp.maximum(b_last + m_prev, m_loc)
        a = jnp.exp(b_last + m_prev - m_new)
        rr = jnp.exp(m_loc - m_new)
        c_ref[h] = a * c_prev + rr * c_loc
        m_ref[h:h + 1, :] = jnp.broadcast_to(m_new, (1, LANES))


def _mlstm(proj, conv_w, b_if, hnorm_g, *, bsz, seq, d):
    heads, vdim = hnorm_g.shape
    assert vdim == LANES
    nqk2 = conv_w.shape[1]
    qk_dim = nqk2 // (2 * heads)
    width = conv_w.shape[0]
    L = min(MLSTM_CHUNK, seq)
    nc = seq // L
    bif = jnp.zeros((1, LANES), F32).at[0, :2 * heads].set(b_if)
    gate_blk = (nqk2 + 2 * d) // LANES
    return pl.pallas_call(
        functools.partial(_mlstm_body, heads=heads, qk_dim=qk_dim, conv_w=width),
        out_shape=jax.ShapeDtypeStruct((bsz * seq, d), BF16),
        grid=(bsz, nc),
        in_specs=[
            pl.BlockSpec((L, nqk2), lambda b, c: (b * nc + c, 0)),
            pl.BlockSpec((L, d), lambda b, c: (b * nc + c, nqk2 // d)),
            pl.BlockSpec((L, d), lambda b, c: (b * nc + c, nqk2 // d + 1)),
            pl.BlockSpec((L, LANES), lambda b, c: (b * nc + c, gate_blk)),
            pl.BlockSpec((width, nqk2), lambda b, c: (0, 0)),
            pl.BlockSpec((1, LANES), lambda b, c: (0, 0)),
            pl.BlockSpec((1, d), lambda b, c: (0, 0)),
        ],
        out_specs=pl.BlockSpec((L, d), lambda b, c: (b * nc + c, 0)),
        scratch_shapes=[
            pltpu.VMEM((SUBLANES, nqk2), F32),
            pltpu.VMEM((heads, LANES, 2 * LANES), F32),
            pltpu.VMEM((heads, LANES), F32),
        ],
        compiler_params=_cparams(("arbitrary", "arbitrary")),
        name="mlstm",
    )(proj, proj, proj, proj, conv_w, bif, hnorm_g.reshape(1, d))


def _sb_body(q_ref, k_ref, v_ref, o_ref, *, head_dim):
    TQ = q_ref.shape[0]
    TK = SB_TILE_K
    kpq = TQ // TK
    npair = q_ref.shape[1] // LANES
    nh = 2 * npair
    qi = pl.program_id(2)
    lane = lax.broadcasted_iota(I32, (TQ, LANES), 1)
    first = lane < head_dim
    q_heads = []
    for p in range(npair):
        q = q_ref[:, p * LANES:(p + 1) * LANES].astype(F32)
        q_heads += [jnp.where(first, q, 0.0).astype(BF16), jnp.where(first, 0.0, q).astype(BF16)]
    r = lax.broadcasted_iota(I32, (TK, TK), 0)
    c = lax.broadcasted_iota(I32, (TK, TK), 1)
    later = jnp.where(r >= c, 1.0, 0.0).astype(BF16)
    row = lax.broadcasted_iota(I32, (TQ, TK), 0)
    col = lax.broadcasted_iota(I32, (TQ, TK), 1)

    def block(kb, carry, dk):
        off = pl.multiple_of(kb * TK, TK)
        strict = None if dk is None else (col + dk * TK) < row
        kblk = [k_ref[pl.ds(off, TK), p * LANES:(p + 1) * LANES] for p in range(npair)]
        vblk = [v_ref[pl.ds(off, TK), p * LANES:(p + 1) * LANES] for p in range(npair)]
        zs = [_dot_tb(q_heads[h], kblk[h // 2]) for h in range(nh)]
        ls = []
        for z in zs:
            l = jnp.maximum(z, 0.0) + jnp.log2(1.0 + jnp.exp2(-jnp.abs(z)))
            ls.append(l if strict is None else jnp.where(strict, l, 0.0))
        tots = [_split_dot(ls[h], later) + carry[2 * h + 1] for h in range(nh)]
        new = []
        for h in range(nh):
            a = jnp.exp2(zs[h] - tots[h])
            if strict is not None:
                a = jnp.where(strict, a, 0.0)
            new += [carry[2 * h] + _dot(a.astype(BF16), vblk[h // 2]),
                    jnp.broadcast_to(tots[h][:, 0:1], (TQ, TK))]
        return tuple(new)

    carry = []
    for _ in range(nh):
        carry += [jnp.zeros((TQ, LANES), F32), jnp.zeros((TQ, TK), F32)]
    carry = tuple(carry)
    for dk in reversed(range(kpq)):
        carry = block(qi * kpq + dk, carry, dk)
    carry = lax.fori_loop(0, qi * kpq, lambda j, cr: block(qi * kpq - 1 - j, cr, None), carry)
    for p in range(npair):
        o_ref[:, p * LANES:(p + 1) * LANES] = jnp.where(first, carry[4 * p], carry[4 * p + 2]).astype(o_ref.dtype)


def _sb_attention(q, k, v, *, bsz, seq, head_dim):
    n, d = q.shape
    assert 2 * head_dim == LANES
    t = min(SB_TILE_Q, seq)
    assert t % SB_TILE_K == 0
    w = min(SB_LANES, d)
    q3, k3, v3 = (a.reshape(bsz, seq, d) for a in (q, k, v))
    out = pl.pallas_call(
        functools.partial(_sb_body, head_dim=head_dim),
        out_shape=jax.ShapeDtypeStruct((bsz, seq, d), BF16),
        grid=(bsz, d // w, seq // t),
        in_specs=[
            pl.BlockSpec((None, t, w), lambda b, hp, i: (b, i, hp)),
            pl.BlockSpec((None, seq, w), lambda b, hp, i: (b, 0, hp)),
            pl.BlockSpec((None, seq, w), lambda b, hp, i: (b, 0, hp)),
        ],
        out_specs=pl.BlockSpec((None, t, w), lambda b, hp, i: (b, i, hp)),
        compiler_params=_cparams(("arbitrary", "arbitrary", "arbitrary")),
        name="sb_attention",
    )(q3, k3, v3)
    return out.reshape(n, d)


def _topk_rows(problems, k, rows=None, big=None):
    problems = list(problems)
    if rows is None:
        rows = lax.broadcasted_iota(I32, problems[0].shape, 0).astype(F32)
        big = float(problems[0].shape[0])
    vals = [[] for _ in problems]
    idxs = [[] for _ in problems]
    for _ in range(k):
        for j, s in enumerate(problems):
            m = jnp.max(s, axis=0, keepdims=True)
            i = jnp.min(jnp.where(s == m, rows, big), axis=0, keepdims=True)
            vals[j].append(m)
            idxs[j].append(i)
            problems[j] = jnp.where(rows == i, -jnp.inf, s)
    return [(jnp.concatenate(v, axis=0), jnp.concatenate(i, axis=0)) for v, i in zip(vals, idxs)]


def _select_rows(table, sel, k):
    out = jnp.zeros_like(sel)
    for a in range(k):
        out = jnp.where(sel == float(a), table[a:a + 1, :], out)
    return out


def _route_body(x_ref, g_ref, sh_ref, sc_ref, wq_ref, keys_ref, h_ref, idx_ref, gate_ref,
                q_scr, idx_t, gate_t, *, heads, nkeys):
    k = PEER_TOPK
    tm = x_ref.shape[0]
    hmod = _rmsnorm_mod(x_ref[...], g_ref[...], sh_ref[...], sc_ref[...])
    h_ref[...] = hmod
    q_scr[...] = _dot(hmod.astype(BF16), wq_ref[...]).astype(BF16)
    nsub = tm // ROUTE_SUB
    ncand = sum(k // (a + 1) for a in range(k))
    cand_pad = (-ncand) % SUBLANES
    ridx = lax.broadcasted_iota(I32, (ncand + cand_pad, ROUTE_SUB), 0)
    cand_pos = jnp.full(ridx.shape, k * k, I32)
    start = 0
    for a in range(k):
        nb = k // (a + 1)
        cand_pos = jnp.where((ridx >= start) & (ridx < start + nb), ridx + (a * k - start), cand_pos)
        start += nb
    cand_pos = cand_pos.astype(F32)

    def head_step(hd, carry):
        scores = []
        for sub in range(nsub):
            for p in range(2):
                c0 = pl.multiple_of((hd * 2 + p) * LANES, LANES)
                qs = q_scr[sub * ROUTE_SUB:(sub + 1) * ROUTE_SUB, pl.ds(c0, LANES)]
                scores.append(_dot_tb(keys_ref[p], qs))
        tops = _topk_rows(scores, k)
        cands = []
        for sub in range(nsub):
            (v0, _), (v1, _) = tops[2 * sub], tops[2 * sub + 1]
            pieces = [v0[a:a + 1, :] + v1[0:k // (a + 1), :] for a in range(k)]
            pieces.append(jnp.full((cand_pad, ROUTE_SUB), -jnp.inf, F32))
            cands.append(jnp.concatenate(pieces, axis=0))
        best = _topk_rows(cands, k, rows=cand_pos, big=float(k * k))
        r0 = pl.multiple_of(hd * k, k)
        for sub in range(nsub):
            (_, i0), (_, i1) = tops[2 * sub], tops[2 * sub + 1]
            g_top, pos = best[sub]
            pa = jnp.floor(pos * (1.0 / k))
            pb = pos - pa * k
            expert = (_select_rows(i0, pa, k) * float(nkeys) + _select_rows(i1, pb, k)) * float(SUBLANES // 2)
            e = jnp.exp(g_top - g_top[0:1, :])
            gate = e / jnp.sum(e, axis=0, keepdims=True)
            idx_t[pl.ds(r0, k), sub * ROUTE_SUB:(sub + 1) * ROUTE_SUB] = expert
            gate_t[pl.ds(r0, k), sub * ROUTE_SUB:(sub + 1) * ROUTE_SUB] = gate
        return carry

    lax.fori_loop(0, heads, head_step, 0)
    idx_ref[...] = idx_t[...].T.astype(I32)
    gate_ref[...] = gate_t[...].T


def _peer_route(x, gain, mod, i_sh, i_sc, wq, keys, *, seq):
    n, d = x.shape
    qcols = wq.shape[1]
    nkeys, kdim = keys.shape[1], keys.shape[2]
    assert kdim == LANES and nkeys == LANES
    heads = qcols // (2 * kdim)
    tm = min(ROUTE_TILE, seq)
    tpb = seq // tm
    ha = heads * PEER_TOPK
    return pl.pallas_call(
        functools.partial(_route_body, heads=heads, nkeys=nkeys),
        out_shape=(jax.ShapeDtypeStruct((n, d), F32),
                   jax.ShapeDtypeStruct((n, ha), I32),
                   jax.ShapeDtypeStruct((n, ha), F32)),
        grid=(n // tm,),
        in_specs=[
            pl.BlockSpec((tm, d), lambda i: (i, 0)),
            pl.BlockSpec((1, d), lambda i: (0, 0)),
            pl.BlockSpec((None, None, 1, d), lambda i: (i // tpb, i_sh, 0, 0)),
            pl.BlockSpec((None, None, 1, d), lambda i: (i // tpb, i_sc, 0, 0)),
            pl.BlockSpec((d, qcols), lambda i: (0, 0)),
            pl.BlockSpec((2, nkeys, kdim), lambda i: (0, 0, 0)),
        ],
        out_specs=(pl.BlockSpec((tm, d), lambda i: (i, 0)),
                   pl.BlockSpec((tm, ha), lambda i: (i, 0)),
                   pl.BlockSpec((tm, ha), lambda i: (i, 0))),
        scratch_shapes=[pltpu.VMEM((tm, qcols), BF16),
                        pltpu.VMEM((ha, tm), F32),
                        pltpu.VMEM((ha, tm), F32)],
        compiler_params=_cparams(("arbitrary",)),
        name="peer_route",
    )(x, gain, mod, mod, wq, keys)


def _pack_table(t):
    e, d = t.shape
    assert d == SUBLANES * LANES
    half = SUBLANES // 2
    bits = lax.bitcast_convert_type(t.astype(BF16).reshape(e, 2, half, LANES), jnp.uint16).astype(jnp.uint32)
    words = lax.bitcast_convert_type(bits[:, 0] | (bits[:, 1] << 16), I32).reshape(e * half, LANES)
    return jnp.pad(words, ((half, half), (0, 0)))


def _row_of(idx_refs, t, a, na):
    return idx_refs[a % IDX_SPLIT][0, t * (na // IDX_SPLIT) + a // IDX_SPLIT]


def _window_pair(tbl_ref, row_a, row_b, low_half):
    half = SUBLANES // 2
    wa = tbl_ref[pl.ds(row_a + half, SUBLANES), :]
    wb = tbl_ref[pl.ds(row_b, SUBLANES), :]
    return jnp.where(low_half, wa, wb)


def _unpack(w):
    lo = lax.bitcast_convert_type(w << 16, F32)
    hi = lax.bitcast_convert_type(w & jnp.int32(-65536), F32)
    return lo, hi


def _merge(v1, v2, g, first_half):
    h = g // 2
    a = v1 + pltpu.roll(v1, SUBLANES - h, axis=0)
    b = v2 + pltpu.roll(v2, h, axis=0)
    return jnp.where(first_half, a, b)


def _erf(x):
    ax = jnp.abs(x)
    t = 1.0 / (1.0 + 0.3275911 * ax)
    poly = ((((1.061405429 * t - 1.453152027) * t + 1.421413741) * t - 0.284496736) * t + 0.254829592) * t
    y = 1.0 - poly * jnp.exp(-ax * ax)
    return jnp.where(x < 0, -y, y)


def _peer_u_body(*refs):
    idx_refs = refs[:IDX_SPLIT]
    tbl_ref, x_ref, g_ref, w_ref = refs[IDX_SPLIT:IDX_SPLIT + 4]
    parts = refs[IDX_SPLIT + 4:]
    tt, na = g_ref.shape
    half = SUBLANES // 2
    sub = lax.broadcasted_iota(I32, (SUBLANES, LANES), 0)
    low_half = sub < half
    first4 = (sub & 3) < 2
    first2 = (sub & 1) < 1
    sub4 = [(sub & 3) == k for k in range(3)]

    def finish(t, part_ref):
        act = jnp.sum(part_ref[...].T, axis=0, keepdims=True)
        gelu = 0.5 * act * (1.0 + _erf(act * (2.0 ** -0.5)))
        return g_ref[pl.ds(t, 1), :] * gelu

    def gather(t, part_ref):
        xrow = x_ref[pl.ds(t, 1), :]
        xc = [jnp.broadcast_to(xrow[:, c * LANES:(c + 1) * LANES], (SUBLANES, LANES)) for c in range(SUBLANES)]
        xl = jnp.where(sub4[0], xc[0], jnp.where(sub4[1], xc[1], jnp.where(sub4[2], xc[2], xc[3])))
        xh = jnp.where(sub4[0], xc[4], jnp.where(sub4[1], xc[5], jnp.where(sub4[2], xc[6], xc[7])))
        groups = []
        for grp in range(na // SUBLANES):
            c = []
            for i in range(half):
                a0 = grp * SUBLANES + BITREV8[2 * i]
                a1 = grp * SUBLANES + BITREV8[2 * i + 1]
                lo, hi = _unpack(_window_pair(tbl_ref, _row_of(idx_refs, t, a0, na), _row_of(idx_refs, t, a1, na),
                                              low_half))
                c.append(lo * xl + hi * xh)
            d0 = _merge(c[0], c[1], 4, first4)
            d1 = _merge(c[2], c[3], 4, first4)
            groups.append(_merge(d0, d1, 2, first2))
        part_ref[...] = jnp.concatenate(groups, axis=0)

    grp = len(parts)

    def token_group(i, carry):
        t = grp * i
        done = [finish(t - grp + k, parts[k]) for k in range(grp)]
        for k in range(grp):
            gather(t + k, parts[k])
        for k in range(grp):
            w_ref[pl.ds(t - grp + k, 1), :] = done[k]
        return carry

    for k in range(grp):
        gather(k, parts[k])
    lax.fori_loop(1, tt // grp, token_group, 0)
    for k in range(grp):
        w_ref[pl.ds(tt - grp + k, 1), :] = finish(tt - grp + k, parts[k])


def _peer_v_body(*refs):
    idx_refs = refs[:IDX_SPLIT]
    tbl_ref, rep_ref, keep_ref, w_ref, res_ref, gate_ref, o_ref, spread_ref = refs[IDX_SPLIT:IDX_SPLIT + 8]
    rows = refs[IDX_SPLIT + 8:]
    tt = res_ref.shape[0]
    na = w_ref.shape[1]
    half = SUBLANES // 2
    low_half = lax.broadcasted_iota(I32, (SUBLANES, LANES), 0) < half

    w_all = w_ref[...]
    w_hi = w_all.astype(BF16)
    w_lo = (w_all - w_hi.astype(F32)).astype(BF16)
    spread_ref[...] = _dot(jnp.concatenate([w_hi, w_lo], axis=0), rep_ref[...])
    keep = keep_ref[...]

    def gather(t, rows_ref):
        for j in range(na // 2):
            tile = _window_pair(tbl_ref, _row_of(idx_refs, t, 2 * j, na), _row_of(idx_refs, t, 2 * j + 1, na),
                                low_half)
            rows_ref[j * SUBLANES:(j + 1) * SUBLANES, :] = tile

    def finish(t, rows_ref):
        lhs = jnp.concatenate([spread_ref[pl.ds(t, 1), :] * keep, spread_ref[pl.ds(tt + t, 1), :] * keep],
                              axis=0).astype(BF16)
        out = _dot(lhs, pltpu.bitcast(rows_ref[...], BF16))
        y = out[0:SUBLANES] + out[SUBLANES:]
        yrow = jnp.concatenate([y[c:c + 1, :] for c in range(SUBLANES)], axis=1)
        return res_ref[pl.ds(t, 1), :] + gate_ref[...] * yrow

    def store(t, row):
        o_ref[pl.ds(t, 1), :] = row

    grp = len(rows)

    def token_group(i, carry):
        t = grp * i
        done = [finish(t - grp + k, rows[k]) for k in range(grp)]
        for k in range(grp):
            gather(t + k, rows[k])
        for k in range(grp):
            store(t - grp + k, done[k])
        return carry

    for k in range(grp):
        gather(k, rows[k])
    lax.fori_loop(1, tt // grp, token_group, 0)
    for k in range(grp):
        store(tt - grp + k, finish(tt - grp + k, rows[k]))


def _v_side_constants(na):
    kk = np.arange(na * SUBLANES)
    rep = (np.arange(na)[:, None] == (2 * (kk // 16) + (kk % 16) // 8)[None, :]).astype(np.float32)
    keep = (np.arange(SUBLANES)[:, None] == (((kk % 16) // 2) % 4 + 4 * (kk % 2))[None, :]).astype(np.float32)
    return jnp.asarray(rep, BF16), jnp.asarray(keep, F32)


def _peer_experts(x, hmod, idx, gate, tbl_u, tbl_v, mod, i_gate, *, seq):
    n, d = x.shape
    na = idx.shape[1]
    tt = min(PEER_TILE, seq)
    tpb = seq // tt
    tbl_spec = pl.BlockSpec(memory_space=pltpu.VMEM)
    per_token = na // IDX_SPLIT
    idx_split = idx.reshape(n, per_token, IDX_SPLIT).transpose(2, 0, 1).reshape(IDX_SPLIT, n // tt, 1, tt * per_token)
    idx_specs = [pl.BlockSpec((None, None, 1, tt * per_token), functools.partial(lambda k, i: (k, i, 0, 0), k),
                              memory_space=pltpu.SMEM) for k in range(IDX_SPLIT)]
    idx_args = [idx_split] * IDX_SPLIT
    w = pl.pallas_call(
        _peer_u_body,
        out_shape=jax.ShapeDtypeStruct((n, na), F32),
        grid=(n // tt,),
        in_specs=idx_specs + [
            tbl_spec,
            pl.BlockSpec((tt, d), lambda i: (i, 0)),
            pl.BlockSpec((tt, na), lambda i: (i, 0)),
        ],
        out_specs=pl.BlockSpec((tt, na), lambda i: (i, 0)),
        scratch_shapes=[pltpu.VMEM((na, LANES), F32)] * PEER_GROUP,
        compiler_params=_cparams(("arbitrary",)),
        name="peer_u",
    )(*idx_args, tbl_u, hmod, gate)
    rep, keep = _v_side_constants(na)
    whole = lambda a: pl.BlockSpec(a.shape, lambda i: (0, 0))
    out = pl.pallas_call(
        _peer_v_body,
        out_shape=jax.ShapeDtypeStruct((n, d), F32),
        grid=(n // tt,),
        in_specs=idx_specs + [
            tbl_spec,
            whole(rep), whole(keep),
            pl.BlockSpec((tt, na), lambda i: (i, 0)),
            pl.BlockSpec((tt, d), lambda i: (i, 0)),
            pl.BlockSpec((None, None, 1, d), lambda i: (i // tpb, i_gate, 0, 0)),
        ],
        out_specs=pl.BlockSpec((tt, d), lambda i: (i, 0)),
        scratch_shapes=[pltpu.VMEM((2 * tt, na * SUBLANES), F32)]
        + [pltpu.VMEM((na * SUBLANES // 2, LANES), I32)] * PEER_GROUP,
        compiler_params=_cparams(("arbitrary",)),
        name="peer_v",
    )(*idx_args, tbl_v, rep, keep, w, x, mod)
    return out


def _peer_ffn(x, gain, mod, wq, sub_keys, u, v, *, seq):
    hmod, idx, gate = _peer_route(x, gain, mod, 3, 4, wq.astype(BF16), sub_keys.astype(BF16), seq=seq)
    return _peer_experts(x, hmod, idx, gate, _pack_table(u), _pack_table(v), mod, 5, seq=seq)


def kernel(x, c, ada_w, ada_b, norm_mix_g, norm_ffn_g, ma_w_in, ma_conv_w, ma_b_if, ma_hnorm_g, ma_w_out,
           kv_ada_w, kv_ada_b, kv_norm_g, kv_w, k_norm_g, sb_w_q, sb_q_norm_g, sb_w_out,
           peer_w_q, peer_sub_keys, peer_u, peer_v):
    bsz, seq, d = x.shape
    n = bsz * seq
    depth = ada_w.shape[0]
    n_a = ma_w_in.shape[0]
    sb_dim = k_norm_g.shape[0]
    sb_heads = d // sb_dim

    mods = _ada(c, ada_w, ada_b).reshape(depth, bsz, 6, 1, d)
    kv_mod = _ada(c, kv_ada_w[None], kv_ada_b[None]).reshape(bsz, 2, 1, d)
    xs = x.reshape(n, d)
    k_sh = v_sh = None
    for layer in range(depth):
        mod = mods[layer]
        g_mix = norm_mix_g[layer].reshape(1, d)
        if layer < n_a:
            w_in = ma_w_in[layer]
            pad = (-w_in.shape[1]) % LANES
            w_in = jnp.pad(w_in, ((0, 0), (0, pad))).astype(BF16)
            proj = _nmm(xs, w_in, seq=seq, norm=(g_mix, mod, 0, 1), name="mlstm_proj")
            hh = _mlstm(proj, ma_conv_w[layer], ma_b_if[layer], ma_hnorm_g[layer], bsz=bsz, seq=seq, d=d)
            xs = _nmm(hh, ma_w_out[layer].astype(BF16), seq=seq, resid=(xs, mod, 2), name="mlstm_out")
        else:
            if layer == n_a:
                kvw = kv_w.astype(BF16)
                k_gain = jnp.tile(k_norm_g, sb_heads).reshape(1, d)
                g_kv = kv_norm_g.reshape(1, d)
                k_sh = _nmm(xs, kvw[:, :d], seq=seq, norm=(g_kv, kv_mod, 0, 1),
                            headnorm=(k_gain, sb_dim), out_dtype=BF16, name="kv_k")
                v_sh = _nmm(xs, kvw[:, d:], seq=seq, norm=(g_kv, kv_mod, 0, 1), out_dtype=BF16, name="kv_v")
            j = layer - n_a
            q_gain = (jnp.tile(sb_q_norm_g[j], sb_heads) * (sb_dim ** -0.5 * LOG2E)).reshape(1, d)
            q = _nmm(xs, sb_w_q[j].astype(BF16), seq=seq, norm=(g_mix, mod, 0, 1),
                     headnorm=(q_gain, sb_dim), out_dtype=BF16, name="sb_q")
            o = _sb_attention(q, k_sh, v_sh, bsz=bsz, seq=seq, head_dim=sb_dim)
            xs = _nmm(o, sb_w_out[j].astype(BF16), seq=seq, resid=(xs, mod, 2), name="sb_out")
        xs = _peer_ffn(xs, norm_ffn_g[layer].reshape(1, d), mod, peer_w_q[layer], peer_sub_keys[layer],
                       peer_u[layer], peer_v[layer], seq=seq)
    return xs.reshape(bsz, seq, d)
```

```python
import functools

import jax
import jax.numpy as jnp
import numpy as np
from jax import lax
from jax.experimental import pallas as pl
from jax.experimental.pallas import tpu as pltpu

F32 = jnp.float32
BF16 = jnp.bfloat16
I32 = jnp.int32

EPS = 1e-6
PEER_TOPK = 16
LANES = 128
SUBLANES = 8
VMEM_LIMIT_BYTES = 56 * 1024 * 1024

MLSTM_CHUNK = 256
SB_TILE_Q = 256
SB_TILE_K = 256
SB_LANES = 512
LOG2E = 1.4426950408889634
MM_TILE_M = 512
ROUTE_TILE = 1024
ROUTE_SUB = 128
PEER_TILE = 256
PEER_GROUP = 16
IDX_SPLIT = 8
BITREV8 = (0, 4, 2, 6, 1, 5, 3, 7)


def _cparams(sem):
    return pltpu.CompilerParams(dimension_semantics=sem, vmem_limit_bytes=VMEM_LIMIT_BYTES)


def _dot(a, b):
    return jnp.dot(a, b, preferred_element_type=F32)


def _dot_tb(a, b):
    return lax.dot_general(a, b, (((1,), (1,)), ((), ())), preferred_element_type=F32)


def _split_dot(x, m):
    hi = x.astype(BF16)
    lo = (x - hi.astype(F32)).astype(BF16)
    return _dot(hi, m) + _dot(lo, m)


def _sigmoid(x):
    return 1.0 / (1.0 + jnp.exp(-x))


def _softplus(x):
    return jnp.maximum(x, 0.0) + jnp.log1p(jnp.exp(-jnp.abs(x)))


def _rmsnorm_mod(x, g, shift, scale):
    ms = jnp.mean(x * x, axis=-1, keepdims=True)
    y = x * lax.rsqrt(ms + EPS) * g
    return y * (1.0 + scale) + shift


def _ada_body(c_ref, w_ref, b_ref, o_ref):
    c = c_ref[...]
    a = (c * _sigmoid(c)).astype(BF16)
    o_ref[...] = _dot(a, w_ref[...].astype(BF16)) + b_ref[...]


def _ada(c, w, b):
    nl, d, e = w.shape
    bsz = c.shape[0]
    te = 1024
    return pl.pallas_call(
        _ada_body,
        out_shape=jax.ShapeDtypeStruct((nl, bsz, e), F32),
        grid=(nl, e // te),
        in_specs=[
            pl.BlockSpec((bsz, d), lambda l, j: (0, 0)),
            pl.BlockSpec((None, d, te), lambda l, j: (l, 0, j)),
            pl.BlockSpec((None, 1, te), lambda l, j: (l, 0, j)),
        ],
        out_specs=pl.BlockSpec((None, bsz, te), lambda l, j: (l, 0, j)),
        compiler_params=_cparams(("arbitrary", "arbitrary")),
        name="ada_mod",
    )(c, w, b.reshape(nl, 1, e))


def _nmm_body(*refs, prologue, epilogue, head_dim):
    it = iter(refs)
    x_ref = next(it)
    if prologue:
        g_ref, sh_ref, sc_ref = next(it), next(it), next(it)
    w_ref = next(it)
    if epilogue == "resid":
        res_ref, gate_ref = next(it), next(it)
    if epilogue == "headnorm":
        hg_ref = next(it)
    o_ref = next(it)
    h_ref = next(it)

    @pl.when(pl.program_id(1) == 0)
    def _():
        x = x_ref[...]
        if prologue:
            x = _rmsnorm_mod(x, g_ref[...], sh_ref[...], sc_ref[...])
        h_ref[...] = x.astype(BF16)

    acc = _dot(h_ref[...], w_ref[...])
    if epilogue == "resid":
        acc = res_ref[...] + gate_ref[...] * acc
    elif epilogue == "headnorm":
        r = lax.broadcasted_iota(I32, (LANES, LANES), 0) // head_dim
        c = lax.broadcasted_iota(I32, (LANES, LANES), 1) // head_dim
        group = jnp.where(r == c, 1.0, 0.0).astype(BF16)
        sq = acc * acc
        ms = jnp.concatenate(
            [_split_dot(sq[:, j * LANES:(j + 1) * LANES], group) for j in range(acc.shape[1] // LANES)],
            axis=1) * (1.0 / head_dim)
        acc = acc * lax.rsqrt(ms + EPS) * hg_ref[...]
    o_ref[...] = acc.astype(o_ref.dtype)


def _nmm(x, w, *, seq, norm=None, resid=None, headnorm=None, out_dtype=F32, tn=None, name):
    n, d = x.shape
    e = w.shape[1]
    tm = min(MM_TILE_M, seq)
    tn = e if tn is None else tn
    tpb = seq // tm
    args = [x]
    in_specs = [pl.BlockSpec((tm, d), lambda i, j: (i, 0))]
    if norm is not None:
        gain, mod, i_sh, i_sc = norm
        args += [gain, mod, mod]
        in_specs += [
            pl.BlockSpec((1, d), lambda i, j: (0, 0)),
            pl.BlockSpec((None, None, 1, d), lambda i, j: (i // tpb, i_sh, 0, 0)),
            pl.BlockSpec((None, None, 1, d), lambda i, j: (i // tpb, i_sc, 0, 0)),
        ]
    args.append(w)
    in_specs.append(pl.BlockSpec((d, tn), lambda i, j: (0, j)))
    epilogue = None
    head_dim = 0
    if resid is not None:
        res, mod, i_g = resid
        epilogue = "resid"
        args += [res, mod]
        in_specs += [
            pl.BlockSpec((tm, tn), lambda i, j: (i, j)),
            pl.BlockSpec((None, None, 1, tn), lambda i, j: (i // tpb, i_g, 0, j)),
        ]
    if headnorm is not None:
        hg, head_dim = headnorm
        epilogue = "headnorm"
        args.append(hg)
        in_specs.append(pl.BlockSpec((1, tn), lambda i, j: (0, j)))
    return pl.pallas_call(
        functools.partial(_nmm_body, prologue=norm is not None, epilogue=epilogue, head_dim=head_dim),
        out_shape=jax.ShapeDtypeStruct((n, e), out_dtype),
        grid=(n // tm, e // tn),
        in_specs=in_specs,
        out_specs=pl.BlockSpec((tm, tn), lambda i, j: (i, j)),
        scratch_shapes=[pltpu.VMEM((tm, d), BF16)],
        compiler_params=_cparams(("arbitrary", "arbitrary")),
        name=name,
    )(*args)


def _mlstm_body(qk_ref, v_ref, o_ref, gt_ref, cw_ref, bif_ref, hg_ref, out_ref,
                tail_ref, c_ref, m_ref, *, heads, qk_dim, conv_w):
    L = qk_ref.shape[0]
    nqk = heads * qk_dim
    vdim = LANES

    @pl.when(pl.program_id(1) == 0)
    def _():
        tail_ref[...] = jnp.zeros_like(tail_ref)
        c_ref[...] = jnp.zeros_like(c_ref)
        m_ref[...] = jnp.zeros_like(m_ref)

    x = qk_ref[...]
    tail = tail_ref[...]
    rows8 = lax.broadcasted_iota(I32, (SUBLANES, x.shape[1]), 0)
    acc = x * cw_ref[conv_w - 1:conv_w, :]
    for j in range(1, conv_w):
        xs = pltpu.roll(x, j, axis=0)
        head8 = jnp.where(rows8 < j, pltpu.roll(tail, j, axis=0), xs[0:SUBLANES])
        xs = jnp.concatenate([head8, xs[SUBLANES:]], axis=0)
        acc = acc + xs * cw_ref[conv_w - 1 - j:conv_w - j, :]
    tail_ref[...] = x[L - SUBLANES:, :]
    qk = acc * _sigmoid(acc)
    q_all = qk[:, :nqk] * (qk_dim ** -0.5)
    k_all = qk[:, nqk:]
    kt_all = k_all.T.astype(BF16)

    gates = gt_ref[...] + bif_ref[...]
    log_f = -_softplus(-gates)
    r = lax.broadcasted_iota(I32, (L, L), 0)
    c = lax.broadcasted_iota(I32, (L, L), 1)
    causal = r >= c
    tri = jnp.where(causal, 1.0, 0.0).astype(BF16)
    lf_hi = log_f.astype(BF16)
    lf_lo = (log_f - lf_hi.astype(F32)).astype(BF16)
    bcum = _dot(tri, lf_hi) + _dot(tri, lf_lo)
    gates_t = gates.T
    bcum_t = bcum.T
    lane = lax.broadcasted_iota(I32, (L, LANES), 1)
    ones_v = jnp.ones((L, vdim), BF16)
    rows_c = lax.broadcasted_iota(I32, (LANES, 2 * vdim), 0)

    for h in range(heads):
        pair, half = h // 2, h % 2
        b_col = bcum[:, heads + h:heads + h + 1]
        i_col = gates[:, h:h + 1]
        b_row = bcum_t[heads + h:heads + h + 1, :]
        i_row = gates_t[h:h + 1, :]
        b_last = b_col[L - 1:L, :]
        m_prev = m_ref[h:h + 1, 0:1]
        in_head = (lane >= half * qk_dim) & (lane < (half + 1) * qk_dim)
        q_h = jnp.where(in_head, q_all[:, pair * LANES:(pair + 1) * LANES], 0.0).astype(BF16)
        kt_h = kt_all[pair * LANES:(pair + 1) * LANES, :]
        v_h = v_ref[:, h * vdim:(h + 1) * vdim]
        v_aug = jnp.concatenate([v_h.astype(BF16), ones_v], axis=1)

        log_d = jnp.where(causal, b_col - b_row + i_row, -jnp.inf)
        inter_log = b_col + m_prev
        m_t = jnp.maximum(inter_log, jnp.max(log_d, axis=1, keepdims=True))
        inter_w = jnp.exp(inter_log - m_t)
        s = _dot(q_h, kt_h) * jnp.exp(log_d - m_t)
        c_prev = c_ref[h]
        tot = _dot(s.astype(BF16), v_aug) + inter_w * _dot(q_h, c_prev.astype(BF16))
        num = tot[:, :vdim]
        den = tot[:, vdim:]
        hh = num / jnp.maximum(jnp.abs(den), jnp.exp(-m_t))
        ms = jnp.mean(hh * hh, axis=1, keepdims=True)
        hh = hh * lax.rsqrt(ms + EPS) * hg_ref[:, h * vdim:(h + 1) * vdim]
        hh = hh * _sigmoid(o_ref[:, h * vdim:(h + 1) * vdim])
        out_ref[:, h * vdim:(h + 1) * vdim] = hh.astype(out_ref.dtype)

        w_state = b_last - b_col + i_col
        m_loc = jnp.max(w_state, axis=0, keepdims=True)
        e_state = jnp.exp(w_state - m_loc)
        ev = (e_state * jnp.concatenate([v_h, jnp.ones((L, vdim), F32)], axis=1)).astype(BF16)
        c_loc = _dot(kt_h, ev)
        in_rows = (rows_c >= half * qk_dim) & (rows_c < (half + 1) * qk_dim)
        c_loc = jnp.where(in_rows, c_loc, 0.0)
        m_new = jnp.maximum(b_last + m_prev, m_loc)
        a = jnp.exp(b_last + m_prev - m_new)
        rr = jnp.exp(m_loc - m_new)
        c_ref[h] = a * c_prev + rr * c_loc
        m_ref[h:h + 1, :] = jnp.broadcast_to(m_new, (1, LANES))


def _mlstm(proj, conv_w, b_if, hnorm_g, *, bsz, seq, d):
    heads, vdim = hnorm_g.shape
    assert vdim == LANES
    nqk2 = conv_w.shape[1]
    qk_dim = nqk2 // (2 * heads)
    width = conv_w.shape[0]
    L = min(MLSTM_CHUNK, seq)
    nc = seq // L
    bif = jnp.zeros((1, LANES), F32).at[0, :2 * heads].set(b_if)
    gate_blk = (nqk2 + 2 * d) // LANES
    return pl.pallas_call(
        functools.partial(_mlstm_body, heads=heads, qk_dim=qk_dim, conv_w=width),
        out_shape=jax.ShapeDtypeStruct((bsz * seq, d), BF16),
        grid=(bsz, nc),
        in_specs=[
            pl.BlockSpec((L, nqk2), lambda b, c: (b * nc + c, 0)),
            pl.BlockSpec((L, d), lambda b, c: (b * nc + c, nqk2 // d)),
            pl.BlockSpec((L, d), lambda b, c: (b * nc + c, nqk2 // d + 1)),
            pl.BlockSpec((L, LANES), lambda b, c: (b * nc + c, gate_blk)),
            pl.BlockSpec((width, nqk2), lambda b, c: (0, 0)),
            pl.BlockSpec((1, LANES), lambda b, c: (0, 0)),
            pl.BlockSpec((1, d), lambda b, c: (0, 0)),
        ],
        out_specs=pl.BlockSpec((L, d), lambda b, c: (b * nc + c, 0)),
        scratch_shapes=[
            pltpu.VMEM((SUBLANES, nqk2), F32),
            pltpu.VMEM((heads, LANES, 2 * LANES), F32),
            pltpu.VMEM((heads, LANES), F32),
        ],
        compiler_params=_cparams(("arbitrary", "arbitrary")),
        name="mlstm",
    )(proj, proj, proj, proj, conv_w, bif, hnorm_g.reshape(1, d))


def _sb_body(q_ref, k_ref, v_ref, o_ref, *, head_dim):
    TQ = q_ref.shape[0]
    TK = SB_TILE_K
    kpq = TQ // TK
    npair = q_ref.shape[1] // LANES
    nh = 2 * npair
    qi = pl.program_id(2)
    lane = lax.broadcasted_iota(I32, (TQ, LANES), 1)
    first = lane < head_dim
    q_heads = []
    for p in range(npair):
        q = q_ref[:, p * LANES:(p + 1) * LANES].astype(F32)
        q_heads += [jnp.where(first, q, 0.0).astype(BF16), jnp.where(first, 0.0, q).astype(BF16)]
    r = lax.broadcasted_iota(I32, (TK, TK), 0)
    c = lax.broadcasted_iota(I32, (TK, TK), 1)
    later = jnp.where(r >= c, 1.0, 0.0).astype(BF16)
    row = lax.broadcasted_iota(I32, (TQ, TK), 0)
    col = lax.broadcasted_iota(I32, (TQ, TK), 1)

    def block(kb, carry, dk):
        off = pl.multiple_of(kb * TK, TK)
        strict = None if dk is None else (col + dk * TK) < row
        kblk = [k_ref[pl.ds(off, TK), p * LANES:(p + 1) * LANES] for p in range(npair)]
        vblk = [v_ref[pl.ds(off, TK), p * LANES:(p + 1) * LANES] for p in range(npair)]
        zs = [_dot_tb(q_heads[h], kblk[h // 2]) for h in range(nh)]
        ls = []
        for z in zs:
            l = jnp.maximum(z, 0.0) + jnp.log2(1.0 + jnp.exp2(-jnp.abs(z)))
            ls.append(l if strict is None else jnp.where(strict, l, 0.0))
        tots = [_split_dot(ls[h], later) + carry[2 * h + 1] for h in range(nh)]
        new = []
        for h in range(nh):
            a = jnp.exp2(zs[h] - tots[h])
            if strict is not None:
                a = jnp.where(strict, a, 0.0)
            new += [carry[2 * h] + _dot(a.astype(BF16), vblk[h // 2]),
                    jnp.broadcast_to(tots[h][:, 0:1], (TQ, TK))]
        return tuple(new)

    carry = []
    for _ in range(nh):
        carry += [jnp.zeros((TQ, LANES), F32), jnp.zeros((TQ, TK), F32)]
    carry = tuple(carry)
    for dk in reversed(range(kpq)):
        carry = block(qi * kpq + dk, carry, dk)
    carry = lax.fori_loop(0, qi * kpq, lambda j, cr: block(qi * kpq - 1 - j, cr, None), carry)
    for p in range(npair):
        o_ref[:, p * LANES:(p + 1) * LANES] = jnp.where(first, carry[4 * p], carry[4 * p + 2]).astype(o_ref.dtype)


def _sb_attention(q, k, v, *, bsz, seq, head_dim):
    n, d = q.shape
    assert 2 * head_dim == LANES
    t = min(SB_TILE_Q, seq)
    assert t % SB_TILE_K == 0
    w = min(SB_LANES, d)
    q3, k3, v3 = (a.reshape(bsz, seq, d) for a in (q, k, v))
    out = pl.pallas_call(
        functools.partial(_sb_body, head_dim=head_dim),
        out_shape=jax.ShapeDtypeStruct((bsz, seq, d), BF16),
        grid=(bsz, d // w, seq // t),
        in_specs=[
            pl.BlockSpec((None, t, w), lambda b, hp, i: (b, i, hp)),
            pl.BlockSpec((None, seq, w), lambda b, hp, i: (b, 0, hp)),
            pl.BlockSpec((None, seq, w), lambda b, hp, i: (b, 0, hp)),
        ],
        out_specs=pl.BlockSpec((None, t, w), lambda b, hp, i: (b, i, hp)),
        compiler_params=_cparams(("arbitrary", "arbitrary", "arbitrary")),
        name="sb_attention",
    )(q3, k3, v3)
    return out.reshape(n, d)


def _topk_rows(problems, k, rows=None, big=None):
    problems = list(problems)
    if rows is None:
        rows = lax.broadcasted_iota(I32, problems[0].shape, 0).astype(F32)
        big = float(problems[0].shape[0])
    vals = [[] for _ in problems]
    idxs = [[] for _ in problems]
    for _ in range(k):
        for j, s in enumerate(problems):
            m = jnp.max(s, axis=0, keepdims=True)
            i = jnp.min(jnp.where(s == m, rows, big), axis=0, keepdims=True)
            vals[j].append(m)
            idxs[j].append(i)
            problems[j] = jnp.where(rows == i, -jnp.inf, s)
    return [(jnp.concatenate(v, axis=0), jnp.concatenate(i, axis=0)) for v, i in zip(vals, idxs)]


def _select_rows(table, sel, k):
    out = jnp.zeros_like(sel)
    for a in range(k):
        out = jnp.where(sel == float(a), table[a:a + 1, :], out)
    return out


def _route_body(x_ref, g_ref, sh_ref, sc_ref, wq_ref, keys_ref, h_ref, idx_ref, gate_ref,
                q_scr, idx_t, gate_t, *, heads, nkeys):
    k = PEER_TOPK
    tm = x_ref.shape[0]
    hmod = _rmsnorm_mod(x_ref[...], g_ref[...], sh_ref[...], sc_ref[...])
    h_ref[...] = hmod
    q_scr[...] = _dot(hmod.astype(BF16), wq_ref[...]).astype(BF16)
    nsub = tm // ROUTE_SUB
    ncand = sum(k // (a + 1) for a in range(k))
    cand_pad = (-ncand) % SUBLANES
    ridx = lax.broadcasted_iota(I32, (ncand + cand_pad, ROUTE_SUB), 0)
    cand_pos = jnp.full(ridx.shape, k * k, I32)
    start = 0
    for a in range(k):
        nb = k // (a + 1)
        cand_pos = jnp.where((ridx >= start) & (ridx < start + nb), ridx + (a * k - start), cand_pos)
        start += nb
    cand_pos = cand_pos.astype(F32)

    def head_step(hd, carry):
        scores = []
        for sub in range(nsub):
            for p in range(2):
                c0 = pl.multiple_of((hd * 2 + p) * LANES, LANES)
                qs = q_scr[sub * ROUTE_SUB:(sub + 1) * ROUTE_SUB, pl.ds(c0, LANES)]
                scores.append(_dot_tb(keys_ref[p], qs))
        tops = _topk_rows(scores, k)
        cands = []
        for sub in range(nsub):
            (v0, _), (v1, _) = tops[2 * sub], tops[2 * sub + 1]
            pieces = [v0[a:a + 1, :] + v1[0:k // (a + 1), :] for a in range(k)]
            pieces.append(jnp.full((cand_pad, ROUTE_SUB), -jnp.inf, F32))
            cands.append(jnp.concatenate(pieces, axis=0))
        best = _topk_rows(cands, k, rows=cand_pos, big=float(k * k))
        r0 = pl.multiple_of(hd * k, k)
        for sub in range(nsub):
            (_, i0), (_, i1) = tops[2 * sub], tops[2 * sub + 1]
            g_top, pos = best[sub]
            pa = jnp.floor(pos * (1.0 / k))
            pb = pos - pa * k
            expert = (_select_rows(i0, pa, k) * float(nkeys) + _select_rows(i1, pb, k)) * float(SUBLANES // 2)
            e = jnp.exp(g_top - g_top[0:1, :])
            gate = e / jnp.sum(e, axis=0, keepdims=True)
            idx_t[pl.ds(r0, k), sub * ROUTE_SUB:(sub + 1) * ROUTE_SUB] = expert
            gate_t[pl.ds(r0, k), sub * ROUTE_SUB:(sub + 1) * ROUTE_SUB] = gate
        return carry

    lax.fori_loop(0, heads, head_step, 0)
    idx_ref[...] = idx_t[...].T.astype(I32)
    gate_ref[...] = gate_t[...].T


def _peer_route(x, gain, mod, i_sh, i_sc, wq, keys, *, seq):
    n, d = x.shape
    qcols = wq.shape[1]
    nkeys, kdim = keys.shape[1], keys.shape[2]
    assert kdim == LANES and nkeys == LANES
    heads = qcols // (2 * kdim)
    tm = min(ROUTE_TILE, seq)
    tpb = seq // tm
    ha = heads * PEER_TOPK
    return pl.pallas_call(
        functools.partial(_route_body, heads=heads, nkeys=nkeys),
        out_shape=(jax.ShapeDtypeStruct((n, d), F32),
                   jax.ShapeDtypeStruct((n, ha), I32),
                   jax.ShapeDtypeStruct((n, ha), F32)),
        grid=(n // tm,),
        in_specs=[
            pl.BlockSpec((tm, d), lambda i: (i, 0)),
            pl.BlockSpec((1, d), lambda i: (0, 0)),
            pl.BlockSpec((None, None, 1, d), lambda i: (i // tpb, i_sh, 0, 0)),
            pl.BlockSpec((None, None, 1, d), lambda i: (i // tpb, i_sc, 0, 0)),
            pl.BlockSpec((d, qcols), lambda i: (0, 0)),
            pl.BlockSpec((2, nkeys, kdim), lambda i: (0, 0, 0)),
        ],
        out_specs=(pl.BlockSpec((tm, d), lambda i: (i, 0)),
                   pl.BlockSpec((tm, ha), lambda i: (i, 0)),
                   pl.BlockSpec((tm, ha), lambda i: (i, 0))),
        scratch_shapes=[pltpu.VMEM((tm, qcols), BF16),
                        pltpu.VMEM((ha, tm), F32),
                        pltpu.VMEM((ha, tm), F32)],
        compiler_params=_cparams(("arbitrary",)),
        name="peer_route",
    )(x, gain, mod, mod, wq, keys)


def _pack_table(t):
    e, d = t.shape
    assert d == SUBLANES * LANES
    half = SUBLANES // 2
    bits = lax.bitcast_convert_type(t.astype(BF16).reshape(e, 2, half, LANES), jnp.uint16).astype(jnp.uint32)
    words = lax.bitcast_convert_type(bits[:, 0] | (bits[:, 1] << 16), I32).reshape(e * half, LANES)
    return jnp.pad(words, ((half, half), (0, 0)))


def _row_of(idx_refs, t, a, na):
    return idx_refs[a % IDX_SPLIT][0, t * (na // IDX_SPLIT) + a // IDX_SPLIT]


def _window_pair(tbl_ref, row_a, row_b, low_half):
    half = SUBLANES // 2
    wa = tbl_ref[pl.ds(row_a + half, SUBLANES), :]
    wb = tbl_ref[pl.ds(row_b, SUBLANES), :]
    return jnp.where(low_half, wa, wb)


def _unpack(w):
    lo = lax.bitcast_convert_type(w << 16, F32)
    hi = lax.bitcast_convert_type(w & jnp.int32(-65536), F32)
    return lo, hi


def _merge(v1, v2, g, first_half):
    h = g // 2
    a = v1 + pltpu.roll(v1, SUBLANES - h, axis=0)
    b = v2 + pltpu.roll(v2, h, axis=0)
    return jnp.where(first_half, a, b)


def _peer_u_body(*refs):
    idx_refs = refs[:IDX_SPLIT]
    tbl_ref, x_ref, g_ref, w_ref = refs[IDX_SPLIT:IDX_SPLIT + 4]
    parts = refs[IDX_SPLIT + 4:]
    tt, na = g_ref.shape
    half = SUBLANES // 2
    sub = lax.broadcasted_iota(I32, (SUBLANES, LANES), 0)
    low_half = sub < half
    first4 = (sub & 3) < 2
    first2 = (sub & 1) < 1
    sub4 = [(sub & 3) == k for k in range(3)]

    def finish(t, part_ref):
        act = jnp.sum(part_ref[...].T, axis=0, keepdims=True)
        gelu = 0.5 * act * (1.0 + lax.erf(act * (2.0 ** -0.5)))
        return g_ref[pl.ds(t, 1), :] * gelu

    def gather(t, part_ref):
        xrow = x_ref[pl.ds(t, 1), :]
        xc = [jnp.broadcast_to(xrow[:, c * LANES:(c + 1) * LANES], (SUBLANES, LANES)) for c in range(SUBLANES)]
        xl = jnp.where(sub4[0], xc[0], jnp.where(sub4[1], xc[1], jnp.where(sub4[2], xc[2], xc[3])))
        xh = jnp.where(sub4[0], xc[4], jnp.where(sub4[1], xc[5], jnp.where(sub4[2], xc[6], xc[7])))
        groups = []
        for grp in range(na // SUBLANES):
            c = []
            for i in range(half):
                a0 = grp * SUBLANES + BITREV8[2 * i]
                a1 = grp * SUBLANES + BITREV8[2 * i + 1]
                lo, hi = _unpack(_window_pair(tbl_ref, _row_of(idx_refs, t, a0, na), _row_of(idx_refs, t, a1, na),
                                              low_half))
                c.append(lo * xl + hi * xh)
            d0 = _merge(c[0], c[1], 4, first4)
            d1 = _merge(c[2], c[3], 4, first4)
            groups.append(_merge(d0, d1, 2, first2))
        part_ref[...] = jnp.concatenate(groups, axis=0)

    grp = len(parts)

    def token_group(i, carry):
        t = grp * i
        done = [finish(t - grp + k, parts[k]) for k in range(grp)]
        for k in range(grp):
            gather(t + k, parts[k])
        for k in range(grp):
            w_ref[pl.ds(t - grp + k, 1), :] = done[k]
        return carry

    for k in range(grp):
        gather(k, parts[k])
    lax.fori_loop(1, tt // grp, token_group, 0)
    for k in range(grp):
        w_ref[pl.ds(tt - grp + k, 1), :] = finish(tt - grp + k, parts[k])


def _peer_v_body(*refs):
    idx_refs = refs[:IDX_SPLIT]
    tbl_ref, rep_ref, keep_ref, w_ref, res_ref, gate_ref, o_ref, spread_ref = refs[IDX_SPLIT:IDX_SPLIT + 8]
    rows = refs[IDX_SPLIT + 8:]
    tt = res_ref.shape[0]
    na = w_ref.shape[1]
    half = SUBLANES // 2
    low_half = lax.broadcasted_iota(I32, (SUBLANES, LANES), 0) < half

    w_all = w_ref[...]
    w_hi = w_all.astype(BF16)
    w_lo = (w_all - w_hi.astype(F32)).astype(BF16)
    spread_ref[...] = _dot(jnp.concatenate([w_hi, w_lo], axis=0), rep_ref[...])
    keep = keep_ref[...]

    def gather(t, rows_ref):
        for j in range(na // 2):
            tile = _window_pair(tbl_ref, _row_of(idx_refs, t, 2 * j, na), _row_of(idx_refs, t, 2 * j + 1, na),
                                low_half)
            rows_ref[j * SUBLANES:(j + 1) * SUBLANES, :] = tile

    def finish(t, rows_ref):
        lhs = jnp.concatenate([spread_ref[pl.ds(t, 1), :] * keep, spread_ref[pl.ds(tt + t, 1), :] * keep],
                              axis=0).astype(BF16)
        out = _dot(lhs, pltpu.bitcast(rows_ref[...], BF16))
        y = out[0:SUBLANES] + out[SUBLANES:]
        yrow = jnp.concatenate([y[c:c + 1, :] for c in range(SUBLANES)], axis=1)
        return res_ref[pl.ds(t, 1), :] + gate_ref[...] * yrow

    def store(t, row):
        o_ref[pl.ds(t, 1), :] = row

    grp = len(rows)

    def token_group(i, carry):
        t = grp * i
        done = [finish(t - grp + k, rows[k]) for k in range(grp)]
        for k in range(grp):
            gather(t + k, rows[k])
        for k in range(grp):
            store(t - grp + k, done[k])
        return carry

    for k in range(grp):
        gather(k, rows[k])
    lax.fori_loop(1, tt // grp, token_group, 0)
    for k in range(grp):
        store(tt - grp + k, finish(tt - grp + k, rows[k]))


def _v_side_constants(na):
    kk = np.arange(na * SUBLANES)
    rep = (np.arange(na)[:, None] == (2 * (kk // 16) + (kk % 16) // 8)[None, :]).astype(np.float32)
    keep = (np.arange(SUBLANES)[:, None] == (((kk % 16) // 2) % 4 + 4 * (kk % 2))[None, :]).astype(np.float32)
    return jnp.asarray(rep, BF16), jnp.asarray(keep, F32)


def _peer_experts(x, hmod, idx, gate, tbl_u, tbl_v, mod, i_gate, *, seq):
    n, d = x.shape
    na = idx.shape[1]
    tt = min(PEER_TILE, seq)
    tpb = seq // tt
    tbl_spec = pl.BlockSpec(memory_space=pltpu.VMEM)
    per_token = na // IDX_SPLIT
    idx_split = idx.reshape(n, per_token, IDX_SPLIT).transpose(2, 0, 1).reshape(IDX_SPLIT, n // tt, 1, tt * per_token)
    idx_specs = [pl.BlockSpec((None, None, 1, tt * per_token), functools.partial(lambda k, i: (k, i, 0, 0), k),
                              memory_space=pltpu.SMEM) for k in range(IDX_SPLIT)]
    idx_args = [idx_split] * IDX_SPLIT
    w = pl.pallas_call(
        _peer_u_body,
        out_shape=jax.ShapeDtypeStruct((n, na), F32),
        grid=(n // tt,),
        in_specs=idx_specs + [
            tbl_spec,
            pl.BlockSpec((tt, d), lambda i: (i, 0)),
            pl.BlockSpec((tt, na), lambda i: (i, 0)),
        ],
        out_specs=pl.BlockSpec((tt, na), lambda i: (i, 0)),
        scratch_shapes=[pltpu.VMEM((na, LANES), F32)] * PEER_GROUP,
        compiler_params=_cparams(("arbitrary",)),
        name="peer_u",
    )(*idx_args, tbl_u, hmod, gate)
    rep, keep = _v_side_constants(na)
    whole = lambda a: pl.BlockSpec(a.shape, lambda i: (0, 0))
    out = pl.pallas_call(
        _peer_v_body,
        out_shape=jax.ShapeDtypeStruct((n, d), F32),
        grid=(n // tt,),
        in_specs=idx_specs + [
            tbl_spec,
            whole(rep), whole(keep),
            pl.BlockSpec((tt, na), lambda i: (i, 0)),
            pl.BlockSpec((tt, d), lambda i: (i, 0)),
            pl.BlockSpec((None, None, 1, d), lambda i: (i // tpb, i_gate, 0, 0)),
        ],
        out_specs=pl.BlockSpec((tt, d), lambda i: (i, 0)),
        scratch_shapes=[pltpu.VMEM((2 * tt, na * SUBLANES), F32)]
        + [pltpu.VMEM((na * SUBLANES // 2, LANES), I32)] * PEER_GROUP,
        compiler_params=_cparams(("arbitrary",)),
        name="peer_v",
    )(*idx_args, tbl_v, rep, keep, w, x, mod)
    return out


def _peer_ffn(x, gain, mod, wq, sub_keys, u, v, *, seq):
    hmod, idx, gate = _peer_route(x, gain, mod, 3, 4, wq.astype(BF16), sub_keys.astype(BF16), seq=seq)
    return _peer_experts(x, hmod, idx, gate, _pack_table(u), _pack_table(v), mod, 5, seq=seq)


def kernel(x, c, ada_w, ada_b, norm_mix_g, norm_ffn_g, ma_w_in, ma_conv_w, ma_b_if, ma_hnorm_g, ma_w_out,
           kv_ada_w, kv_ada_b, kv_norm_g, kv_w, k_norm_g, sb_w_q, sb_q_norm_g, sb_w_out,
           peer_w_q, peer_sub_keys, peer_u, peer_v):
    bsz, seq, d = x.shape
    n = bsz * seq
    depth = ada_w.shape[0]
    n_a = ma_w_in.shape[0]
    sb_dim = k_norm_g.shape[0]
    sb_heads = d // sb_dim

    mods = _ada(c, ada_w, ada_b).reshape(depth, bsz, 6, 1, d)
    kv_mod = _ada(c, kv_ada_w[None], kv_ada_b[None]).reshape(bsz, 2, 1, d)
    xs = x.reshape(n, d)
    k_sh = v_sh = None
    for layer in range(depth):
        mod = mods[layer]
        g_mix = norm_mix_g[layer].reshape(1, d)
        if layer < n_a:
            w_in = ma_w_in[layer]
            pad = (-w_in.shape[1]) % LANES
            w_in = jnp.pad(w_in, ((0, 0), (0, pad))).astype(BF16)
            proj = _nmm(xs, w_in, seq=seq, norm=(g_mix, mod, 0, 1), name="mlstm_proj")
            hh = _mlstm(proj, ma_conv_w[layer], ma_b_if[layer], ma_hnorm_g[layer], bsz=bsz, seq=seq, d=d)
            xs = _nmm(hh, ma_w_out[layer].astype(BF16), seq=seq, resid=(xs, mod, 2), name="mlstm_out")
        else:
            if layer == n_a:
                kvw = kv_w.astype(BF16)
                k_gain = jnp.tile(k_norm_g, sb_heads).reshape(1, d)
                g_kv = kv_norm_g.reshape(1, d)
                k_sh = _nmm(xs, kvw[:, :d], seq=seq, norm=(g_kv, kv_mod, 0, 1),
                            headnorm=(k_gain, sb_dim), out_dtype=BF16, name="kv_k")
                v_sh = _nmm(xs, kvw[:, d:], seq=seq, norm=(g_kv, kv_mod, 0, 1), out_dtype=BF16, name="kv_v")
            j = layer - n_a
            q_gain = (jnp.tile(sb_q_norm_g[j], sb_heads) * (sb_dim ** -0.5 * LOG2E)).reshape(1, d)
            q = _nmm(xs, sb_w_q[j].astype(BF16), seq=seq, norm=(g_mix, mod, 0, 1),
                     headnorm=(q_gain, sb_dim), out_dtype=BF16, name="sb_q")
            o = _sb_attention(q, k_sh, v_sh, bsz=bsz, seq=seq, head_dim=sb_dim)
            xs = _nmm(o, sb_w_out[j].astype(BF16), seq=seq, resid=(xs, mod, 2), name="sb_out")
        xs = _peer_ffn(xs, norm_ffn_g[layer].reshape(1, d), mod, peer_w_q[layer], peer_sub_keys[layer],
                       peer_u[layer], peer_v[layer], seq=seq)
    return xs.reshape(bsz, seq, d)
```
